```python
import jax, jax.numpy as jnp
from jax import lax
import numpy as np

D_MODEL = 1024
BATCH = 16
SEQ = 4096
DEPTH = 1
DEC_BATCH = 2
DEC_SEQ = 16384
PAST_LEN = 128

HEAD_DIM = 64
N_ATTN_HEADS = 8
ATTN_WIDTH = N_ATTN_HEADS * HEAD_DIM
CONV_WIDTH = D_MODEL - ATTN_WIDTH
IN_PROJ_WIDTH = 3 * ATTN_WIDTH + 3 * CONV_WIDTH
CONV_K = 3
D_FF = 2816
ROPE_THETA = 500000.0
ROT_DIM = HEAD_DIM // 4
DILATED_BRANCHES = ((128, 1), (512, 4), (2048, 16))
Q_BLOCK = 128
NORM_EPS = 1e-6

kernel_name = "hymba_conv_dilated_macaron_encoder"


def rms_norm(x, g):
    xf = x.astype(jnp.float32)
    y = xf * lax.rsqrt(jnp.mean(xf * xf, axis=-1, keepdims=True) + NORM_EPS)
    return (y * g.astype(jnp.float32)).astype(x.dtype)


def swiglu(x, w_gate, w_up, w_down):
    return (jax.nn.silu(x @ w_gate) * (x @ w_up)) @ w_down


def partial_rope(t, pos):
    half = ROT_DIM // 2
    inv_freq = jnp.power(jnp.float32(ROPE_THETA), -jnp.arange(half, dtype=jnp.float32) * 2.0 / ROT_DIM)
    ang = pos.astype(jnp.float32)[:, None] * inv_freq[None, :]
    cos = jnp.cos(ang)[None, :, None, :]
    sin = jnp.sin(ang)[None, :, None, :]
    tf = t.astype(jnp.float32)
    t1 = tf[..., :half]
    t2 = tf[..., half:ROT_DIM]
    out = jnp.concatenate([t1 * cos - t2 * sin, t2 * cos + t1 * sin, tf[..., ROT_DIM:]], axis=-1)
    return out.astype(t.dtype)


def dilated_branch(q, k, v, window, dil):
    B, S, H, Dh = q.shape
    L = S // dil
    half = window // (2 * dil)
    nb = -(-L // Q_BLOCK)
    Lp = nb * Q_BLOCK
    span = Q_BLOCK + 2 * half

    def to_sub(t):
        return t.reshape(B, L, dil, H, Dh).transpose(0, 2, 1, 3, 4).reshape(B * dil, L, H, Dh)

    qs = jnp.pad(to_sub(q), ((0, 0), (0, Lp - L), (0, 0), (0, 0)))
    pad_kv = ((0, 0), (half, Lp - L + half), (0, 0), (0, 0))
    ks = jnp.pad(to_sub(k), pad_kv)
    vs = jnp.pad(to_sub(v), pad_kv)

    blk = jnp.arange(nb)[:, None] * Q_BLOCK
    kidx = blk + jnp.arange(span)[None, :]
    kb = ks[:, kidx].astype(jnp.float32)
    vb = vs[:, kidx].astype(jnp.float32)
    qb = qs.reshape(B * dil, nb, Q_BLOCK, H, Dh).astype(jnp.float32)

    s = jnp.einsum('bnqhd,bnkhd->bnhqk', qb, kb) * (Dh ** -0.5)
    kpos = (kidx - half)[:, None, :]
    qpos = (blk + jnp.arange(Q_BLOCK)[None, :])[:, :, None]
    rel = kpos - qpos
    valid = (jnp.abs(rel) <= half) & (((kpos >= 0) & (kpos < L)) | (rel == 0))
    s = jnp.where(valid[None, :, None], s, -jnp.inf)
    m = jnp.max(s, axis=-1, keepdims=True)
    p = jnp.exp(s - m)
    den = jnp.sum(p, axis=-1, keepdims=True)
    o = jnp.einsum('bnhqk,bnkhd->bnqhd', p, vb) / jnp.transpose(den, (0, 1, 3, 2, 4))
    lse = jnp.transpose((m + jnp.log(den))[..., 0], (0, 1, 3, 2))

    def from_sub(t):
        rest = t.shape[3:]
        t = t.reshape((B * dil, Lp) + rest)[:, :L]
        t = t.reshape((B, dil, L) + rest)
        t = jnp.swapaxes(t, 1, 2)
        return t.reshape((B, S) + rest)

    return from_sub(o), from_sub(lse)


def dilated_mixture_attention(q, k, v):
    outs, lses = [], []
    for window, dil in DILATED_BRANCHES:
        o, l = dilated_branch(q, k, v, window, dil)
        outs.append(o)
        lses.append(l)
    w = jax.nn.softmax(jnp.stack(lses, axis=0), axis=0)
    o = jnp.sum(w[..., None] * jnp.stack(outs, axis=0), axis=0)
    return o.astype(q.dtype)


def short_conv_mixer(b, c, v, conv_w):
    u = c * v
    S = u.shape[1]
    up = jnp.pad(u, ((0, 0), (1, 1), (0, 0)))
    conv = up[:, 0:S] * conv_w[0] + up[:, 1:S + 1] * conv_w[1] + up[:, 2:S + 2] * conv_w[2]
    return b * conv


def encoder_layer(x, ffn1_pre_g, ffn1_w_gate, ffn1_w_up, ffn1_w_down, ffn1_post_g,
                  mix_pre_g, w_in, conv_w, attn_out_g, conv_out_g, w_out, mix_post_g,
                  ffn2_pre_g, ffn2_w_gate, ffn2_w_up, ffn2_w_down, ffn2_post_g):
    B, S, _ = x.shape
    h = x + 0.5 * rms_norm(swiglu(rms_norm(x, ffn1_pre_g), ffn1_w_gate, ffn1_w_up, ffn1_w_down), ffn1_post_g)
    u = rms_norm(h, mix_pre_g)
    z = u @ w_in
    q, k, v, cb, cc, cv = jnp.split(z, 6, axis=-1)
    pos = jnp.arange(S)
    q = partial_rope(q.reshape(B, S, N_ATTN_HEADS, HEAD_DIM), pos)
    k = partial_rope(k.reshape(B, S, N_ATTN_HEADS, HEAD_DIM), pos)
    v = v.reshape(B, S, N_ATTN_HEADS, HEAD_DIM)
    attn = dilated_mixture_attention(q, k, v).reshape(B, S, ATTN_WIDTH)
    conv = short_conv_mixer(cb, cc, cv, conv_w)
    mixed = jnp.concatenate([rms_norm(attn, attn_out_g), rms_norm(conv, conv_out_g)], axis=-1) @ w_out
    h = h + rms_norm(mixed, mix_post_g)
    h = h + 0.5 * rms_norm(swiglu(rms_norm(h, ffn2_pre_g), ffn2_w_gate, ffn2_w_up, ffn2_w_down), ffn2_post_g)
    return h


def setup_inputs(seed: int = 0) -> dict:
    key = jax.random.key(seed)
    ks = jax.random.split(key, 20)
    f32 = jnp.float32

    def nrm(k, shape, scale):
        return jax.random.normal(k, shape, f32) * scale

    def gain(k, n):
        return 1.0 + 0.02 * jax.random.normal(k, (DEPTH, n), f32)

    return {
        "x_prompt": jax.random.normal(ks[0], (BATCH, SEQ, D_MODEL), f32),
        "x_sample": jax.random.normal(ks[1], (DEC_BATCH, DEC_SEQ, D_MODEL), f32),
        "ffn1_pre_g": gain(ks[2], D_MODEL),
        "ffn1_w_gate": nrm(ks[3], (DEPTH, D_MODEL, D_FF), D_MODEL ** -0.5),
        "ffn1_w_up": nrm(ks[4], (DEPTH, D_MODEL, D_FF), D_MODEL ** -0.5),
        "ffn1_w_down": nrm(ks[5], (DEPTH, D_FF, D_MODEL), D_FF ** -0.5),
        "ffn1_post_g": gain(ks[6], D_MODEL),
        "mix_pre_g": gain(ks[7], D_MODEL),
        "w_in": nrm(ks[8], (DEPTH, D_MODEL, IN_PROJ_WIDTH), D_MODEL ** -0.5),
        "conv_w": nrm(ks[9], (DEPTH, CONV_K, CONV_WIDTH), CONV_K ** -0.5),
        "attn_out_g": gain(ks[10], ATTN_WIDTH),
        "conv_out_g": gain(ks[11], CONV_WIDTH),
        "w_out": nrm(ks[12], (DEPTH, D_MODEL, D_MODEL), D_MODEL ** -0.5),
        "mix_post_g": gain(ks[13], D_MODEL),
        "ffn2_pre_g": gain(ks[14], D_MODEL),
        "ffn2_w_gate": nrm(ks[15], (DEPTH, D_MODEL, D_FF), D_MODEL ** -0.5),
        "ffn2_w_up": nrm(ks[16], (DEPTH, D_MODEL, D_FF), D_MODEL ** -0.5),
        "ffn2_w_down": nrm(ks[17], (DEPTH, D_FF, D_MODEL), D_FF ** -0.5),
        "ffn2_post_g": gain(ks[18], D_MODEL),
    }


def reference(x_prompt, x_sample, ffn1_pre_g, ffn1_w_gate, ffn1_w_up, ffn1_w_down, ffn1_post_g,
              mix_pre_g, w_in, conv_w, attn_out_g, conv_out_g, w_out, mix_post_g,
              ffn2_pre_g, ffn2_w_gate, ffn2_w_up, ffn2_w_down, ffn2_post_g):
    y_prompt = x_prompt
    y_sample = x_sample
    for l in range(DEPTH):
        p = (ffn1_pre_g[l], ffn1_w_gate[l], ffn1_w_up[l], ffn1_w_down[l], ffn1_post_g[l],
             mix_pre_g[l], w_in[l], conv_w[l], attn_out_g[l], conv_out_g[l], w_out[l], mix_post_g[l],
             ffn2_pre_g[l], ffn2_w_gate[l], ffn2_w_up[l], ffn2_w_down[l], ffn2_post_g[l])
        y_prompt = encoder_layer(y_prompt, *p)
        y_sample = encoder_layer(y_sample, *p)
    return (y_prompt, y_sample)
```

```python
import functools

import jax
import jax.numpy as jnp
from jax import lax
from jax.experimental import pallas as pl
from jax.experimental.pallas import tpu as pltpu

D_MODEL = 1024
D_FF = 2816
HEAD_DIM = 64
N_HEADS = 8
ATTN_WIDTH = N_HEADS * HEAD_DIM
CONV_WIDTH = D_MODEL - ATTN_WIDTH
ROT_DIM = HEAD_DIM // 4
ROPE_THETA = 500000.0
BRANCHES = ((128, 1), (512, 4), (2048, 16))
NORM_EPS = 1e-6

LANES = 128
BF16_SUBLANES = 16
Q_SUB = 128
N_SLABS = ATTN_WIDTH // LANES
ROW_TILE = 512
ATTN_TILE = 2048
FF_CHUNK = 512
VMEM_LIMIT = 48 * 1024 * 1024

F32 = jnp.float32
BF16 = jnp.bfloat16


def _rms(x, g):
    ms = jnp.mean(x * x, axis=-1, keepdims=True)
    return x * lax.rsqrt(ms + NORM_EPS) * g


def _resident(shape):
    zeros = (0,) * len(shape)
    return pl.BlockSpec(shape, lambda *_: zeros, pipeline_mode=pl.Buffered(1))


def _ffn_kernel(x_ref, pre_g_ref, wg_ref, wu_ref, wd_ref, post_g_ref, o_ref, a_scr):
    x = x_ref[0]
    xn = _rms(x, pre_g_ref[...]).astype(BF16)
    for c0 in range(0, D_FF, FF_CHUNK):
        cw = min(FF_CHUNK, D_FF - c0)
        g = jnp.dot(xn, wg_ref[:, c0:c0 + cw], preferred_element_type=F32)
        u = jnp.dot(xn, wu_ref[:, c0:c0 + cw], preferred_element_type=F32)
        a_scr[:, c0:c0 + cw] = (g * jax.nn.sigmoid(g) * u).astype(BF16)
    y = jnp.dot(a_scr[...], wd_ref[...], preferred_element_type=F32)
    o_ref[0] = x + 0.5 * _rms(y, post_g_ref[...])


def _ffn(x, pre_g, wg, wu, wd, post_g):
    B, S, D = x.shape
    tm = ROW_TILE
    tile = pl.BlockSpec((1, tm, D), lambda b, i: (b, i, 0))
    return pl.pallas_call(
        _ffn_kernel,
        grid=(B, S // tm),
        in_specs=[tile, _resident((1, D)), _resident((D, D_FF)), _resident((D, D_FF)),
                  _resident((D_FF, D)), _resident((1, D))],
        out_specs=tile,
        out_shape=jax.ShapeDtypeStruct((B, S, D), F32),
        scratch_shapes=[pltpu.VMEM((tm, D_FF), BF16)],
        compiler_params=pltpu.CompilerParams(
            dimension_semantics=("parallel", "parallel"), vmem_limit_bytes=VMEM_LIMIT),
        name="ffn",
    )(x, pre_g, wg, wu, wd, post_g)


def _mix_in_kernel(h_ref, g_ref, w_ref, cos_ref, sa_ref, sb_ref, *refs):
    n_br = len(BRANCHES)
    q_refs, k_refs, v_refs = refs[0:n_br], refs[n_br:2 * n_br], refs[2 * n_br:3 * n_br]
    cb_ref, cu_ref, slab_scr = refs[3 * n_br:]
    tm = h_ref.shape[1]
    u = _rms(h_ref[0], g_ref[...]).astype(BF16)

    def proj(section):
        c0 = section * ATTN_WIDTH
        return jnp.dot(u, w_ref[:, c0:c0 + ATTN_WIDTH], preferred_element_type=F32)

    def rope(t):
        half = ROT_DIM // 2
        return (t * cos_ref[...] + pltpu.roll(t, LANES - half, 1) * sa_ref[...]
                + pltpu.roll(t, half, 1) * sb_ref[...])

    def emit(t, out_refs, rotary, scale):
        for s in range(N_SLABS):
            ts = t[:, s * LANES:(s + 1) * LANES]
            if rotary:
                ts = rope(ts)
            if scale != 1.0:
                ts = ts * scale
            slab_scr[s] = ts
        for (_, dil), out in zip(BRANCHES, out_refs):
            rows = tm // dil
            for r in range(dil):
                for s in range(N_SLABS):
                    if dil == 1:
                        blk = slab_scr[s]
                    else:
                        blk = slab_scr[s, pl.ds(r, rows, stride=dil), :]
                    out[0, r, :, s * LANES:(s + 1) * LANES] = blk.astype(BF16)

    emit(proj(0), q_refs, True, HEAD_DIM ** -0.5)
    emit(proj(1), k_refs, True, 1.0)
    emit(proj(2), v_refs, False, 1.0)
    cb_ref[0] = proj(3).astype(BF16)
    cu_ref[0] = (proj(4) * proj(5)).astype(BF16)


def _mix_in(h, g, w_in, cos_t, sa_t, sb_t):
    B, S, D = h.shape
    tm = ROW_TILE
    row = lambda w: pl.BlockSpec((1, tm, w), lambda b, i: (b, i, 0))
    tab = pl.BlockSpec((tm, LANES), lambda b, i: (i, 0))
    sub_shapes, sub_specs = [], []
    for _ in range(3):
        for _, dil in BRANCHES:
            sub_shapes.append(jax.ShapeDtypeStruct((B, dil, S // dil, ATTN_WIDTH), BF16))
            sub_specs.append(pl.BlockSpec((1, dil, tm // dil, ATTN_WIDTH), lambda b, i: (b, 0, i, 0)))
    conv_shape = jax.ShapeDtypeStruct((B, S, CONV_WIDTH), BF16)
    return pl.pallas_call(
        _mix_in_kernel,
        grid=(B, S // tm),
        in_specs=[row(D), _resident((1, D)), _resident((D, w_in.shape[1])), tab, tab, tab],
        out_specs=sub_specs + [row(CONV_WIDTH), row(CONV_WIDTH)],
        out_shape=sub_shapes + [conv_shape, conv_shape],
        scratch_shapes=[pltpu.VMEM((N_SLABS, tm, LANES), F32)],
        compiler_params=pltpu.CompilerParams(
            dimension_semantics=("parallel", "parallel"), vmem_limit_bytes=VMEM_LIMIT),
        name="mix_in",
    )(h, g, w_in, cos_t, sa_t, sb_t)


def _attn_kernel(q_ref, k_ref, kl_ref, kr_ref, v_ref, vl_ref, vr_ref, o_ref, lse_ref, kx, vx,
                 *, dil, half, sub_len):
    i = pl.program_id(1)
    tr = q_ref.shape[2]
    nsub = tr // Q_SUB
    span = Q_SUB + 2 * half

    kx[:, 0:half] = kl_ref[0]
    kx[:, half:half + tr] = k_ref[0]
    kx[:, half + tr:] = kr_ref[0]
    vx[:, 0:half] = vl_ref[0]
    vx[:, half:half + tr] = v_ref[0]
    vx[:, half + tr:] = vr_ref[0]

    row = lax.broadcasted_iota(jnp.int32, (Q_SUB, span), 0)
    col = lax.broadcasted_iota(jnp.int32, (Q_SUB, span), 1)
    band_bias = jnp.where((col >= row) & (col <= row + 2 * half), 0.0, -jnp.inf).astype(F32)
    col1 = lax.broadcasted_iota(jnp.int32, (1, span), 1)
    first = lax.broadcasted_iota(jnp.int32, (Q_SUB, LANES), 1) < HEAD_DIM

    def body(j, carry):
        r = j // nsub
        c = j % nsub
        r0 = pl.multiple_of(c * Q_SUB, Q_SUB)
        base = i * tr + c * Q_SUB
        edge_bias = jnp.where((col1 >= half - base) & (col1 < sub_len + half - base), 0.0, -jnp.inf)
        bias = band_bias + edge_bias.astype(F32)
        q2 = q_ref[0, r, pl.ds(r0, Q_SUB), :]
        k2 = kx[r, pl.ds(r0, span), :]
        v2 = vx[r, pl.ds(r0, span), :]
        outs, lses = [], []
        for hh in range(2):
            msk = first if hh == 0 else jnp.logical_not(first)
            qh = jnp.where(msk, q2, jnp.zeros_like(q2))
            s = lax.dot_general(qh, k2, (((1,), (1,)), ((), ())), preferred_element_type=F32) + bias
            m = jnp.max(s, axis=-1, keepdims=True)
            p = jnp.exp(s - m)
            l = jnp.sum(p, axis=-1, keepdims=True)
            o = jnp.dot(p.astype(BF16), v2, preferred_element_type=F32)
            outs.append(o / l)
            lses.append(jnp.broadcast_to(m + jnp.log(l), (Q_SUB, LANES)))
        start = r0 * dil + r
        rows = pl.ds(start, Q_SUB) if dil == 1 else pl.ds(start, Q_SUB, stride=dil)
        o_ref[0, 0, rows, :] = jnp.where(first, outs[0], outs[1])
        lse_ref[0, 0, rows, :] = jnp.where(first, lses[0], lses[1])
        return carry

    lax.fori_loop(0, dil * nsub, body, 0)


def _attn_branch(q, k, v, window, dil):
    B, _, L, _ = q.shape
    S = L * dil
    half = window // (2 * dil)
    T = min(ATTN_TILE, S)
    tr = T // dil
    assert tr % Q_SUB == 0 and L % tr == 0 and tr % half == 0 and half % BF16_SUBLANES == 0
    hb = tr // half
    n_hblk = L // half
    main = pl.BlockSpec((1, dil, tr, LANES), lambda b, i, s: (b, 0, i, s))
    left = pl.BlockSpec((1, dil, half, LANES), lambda b, i, s: (b, 0, jnp.maximum(i * hb - 1, 0), s))
    right = pl.BlockSpec((1, dil, half, LANES),
                         lambda b, i, s: (b, 0, jnp.minimum((i + 1) * hb, n_hblk - 1), s))
    out = pl.BlockSpec((1, 1, T, LANES), lambda b, i, s: (b, s, i, 0))
    out_shape = jax.ShapeDtypeStruct((B, N_SLABS, S, LANES), F32)
    return pl.pallas_call(
        functools.partial(_attn_kernel, dil=dil, half=half, sub_len=L),
        grid=(B, S // T, N_SLABS),
        in_specs=[main, main, left, right, main, left, right],
        out_specs=[out, out],
        out_shape=[out_shape, out_shape],
        scratch_shapes=[pltpu.VMEM((dil, tr + 2 * half, LANES), BF16)] * 2,
        compiler_params=pltpu.CompilerParams(
            dimension_semantics=("parallel", "parallel", "parallel"), vmem_limit_bytes=VMEM_LIMIT),
        name=f"attn_d{dil}",
    )(q, k, k, k, v, v, v)


def _mix_out_kernel(h_ref, *refs):
    n_br = len(BRANCHES)
    o_refs, lse_refs = refs[0:n_br], refs[n_br:2 * n_br]
    (cb_ref, cu_ref, cul_ref, cur_ref, cw_ref, ag_ref, cg_ref, w_ref, pg_ref,
     out_ref, conv_scr) = refs[2 * n_br:]
    i = pl.program_id(1)
    tm = h_ref.shape[1]

    slabs = []
    for s in range(N_SLABS):
        lses = [ref[0, s] for ref in lse_refs]
        top = functools.reduce(jnp.maximum, lses)
        es = [jnp.exp(l - top) for l in lses]
        num = functools.reduce(lambda a, b: a + b, [e * ref[0, s] for e, ref in zip(es, o_refs)])
        slabs.append(num / functools.reduce(lambda a, b: a + b, es))
    attn = jnp.concatenate(slabs, axis=1)

    pad = 8
    lrow = cul_ref[0].astype(F32)[BF16_SUBLANES - 1:BF16_SUBLANES]
    rrow = cur_ref[0].astype(F32)[0:1]
    conv_scr[pad - 1:pad] = jnp.where(i > 0, lrow, 0.0)
    conv_scr[pad:pad + tm] = cu_ref[0].astype(F32)
    conv_scr[pad + tm:pad + tm + 1] = jnp.where(i < pl.num_programs(1) - 1, rrow, 0.0)
    conv = (conv_scr[pad - 1:pad - 1 + tm] * cw_ref[0:1] + conv_scr[pad:pad + tm] * cw_ref[1:2]
            + conv_scr[pad + 1:pad + 1 + tm] * cw_ref[2:3])
    conv = cb_ref[0].astype(F32) * conv

    cat = jnp.concatenate([_rms(attn, ag_ref[...]), _rms(conv, cg_ref[...])], axis=1).astype(BF16)
    mixed = jnp.dot(cat, w_ref[...], preferred_element_type=F32)
    out_ref[0] = h_ref[0] + _rms(mixed, pg_ref[...])


def _mix_out(h, outs, lses, cb, cu, conv_w, attn_g, conv_g, w_out, post_g):
    B, S, D = h.shape
    tm = ROW_TILE
    hb = tm // BF16_SUBLANES
    n_hblk = S // BF16_SUBLANES
    row = lambda w: pl.BlockSpec((1, tm, w), lambda b, i: (b, i, 0))
    slab = pl.BlockSpec((1, N_SLABS, tm, LANES), lambda b, i: (b, 0, i, 0))
    left = pl.BlockSpec((1, BF16_SUBLANES, CONV_WIDTH), lambda b, i: (b, jnp.maximum(i * hb - 1, 0), 0))
    right = pl.BlockSpec((1, BF16_SUBLANES, CONV_WIDTH),
                         lambda b, i: (b, jnp.minimum((i + 1) * hb, n_hblk - 1), 0))
    n_br = len(BRANCHES)
    return pl.pallas_call(
        _mix_out_kernel,
        grid=(B, S // tm),
        in_specs=[row(D)] + [slab] * (2 * n_br) + [
            row(CONV_WIDTH), row(CONV_WIDTH), left, right, _resident(conv_w.shape),
            _resident((1, ATTN_WIDTH)), _resident((1, CONV_WIDTH)), _resident((D, D)), _resident((1, D))],
        out_specs=row(D),
        out_shape=jax.ShapeDtypeStruct((B, S, D), F32),
        scratch_shapes=[pltpu.VMEM((tm + 16, CONV_WIDTH), F32)],
        compiler_params=pltpu.CompilerParams(
            dimension_semantics=("parallel", "parallel"), vmem_limit_bytes=VMEM_LIMIT),
        name="mix_out",
    )(h, *outs, *lses, cb, cu, cu, cu, conv_w, attn_g, conv_g, w_out, post_g)


def _rope_tables(S):
    half = ROT_DIM // 2
    inv_freq = jnp.power(jnp.float32(ROPE_THETA), -jnp.arange(half, dtype=F32) * 2.0 / ROT_DIM)
    ang = jnp.arange(S).astype(F32)[:, None] * inv_freq[None, :]
    cos, sin = jnp.cos(ang), jnp.sin(ang)
    ones = jnp.ones((S, HEAD_DIM - ROT_DIM), F32)
    zeros = jnp.zeros((S, HEAD_DIM - ROT_DIM), F32)
    zh = jnp.zeros((S, half), F32)
    cos_h = jnp.concatenate([cos, cos, ones], axis=1)
    sa_h = jnp.concatenate([-sin, zh, zeros], axis=1)
    sb_h = jnp.concatenate([zh, sin, zeros], axis=1)
    rep = LANES // HEAD_DIM
    return tuple(jnp.tile(t, (1, rep)) for t in (cos_h, sa_h, sb_h))


def _encoder_layer(x, p):
    S = x.shape[1]
    n_br = len(BRANCHES)
    h = _ffn(x, p["ffn1_pre_g"], p["ffn1_w_gate"], p["ffn1_w_up"], p["ffn1_w_down"], p["ffn1_post_g"])
    res = _mix_in(h, p["mix_pre_g"], p["w_in"], *_rope_tables(S))
    qs, ks, vs = res[0:n_br], res[n_br:2 * n_br], res[2 * n_br:3 * n_br]
    cb, cu = res[3 * n_br:]
    outs, lses = [], []
    for (window, dil), q, k, v in zip(BRANCHES, qs, ks, vs):
        o, l = _attn_branch(q, k, v, window, dil)
        outs.append(o)
        lses.append(l)
    h = _mix_out(h, outs, lses, cb, cu, p["conv_w"], p["attn_out_g"], p["conv_out_g"], p["w_out"],
                 p["mix_post_g"])
    return _ffn(h, p["ffn2_pre_g"], p["ffn2_w_gate"], p["ffn2_w_up"], p["ffn2_w_down"], p["ffn2_post_g"])


_MATRICES = ("ffn1_w_gate", "ffn1_w_up", "ffn1_w_down", "w_in", "w_out", "ffn2_w_gate", "ffn2_w_up",
             "ffn2_w_down")


def kernel(x_prompt, x_sample, ffn1_pre_g, ffn1_w_gate, ffn1_w_up, ffn1_w_down, ffn1_post_g, mix_pre_g, w_in, conv_w, attn_out_g, conv_out_g, w_out, mix_post_g, ffn2_pre_g, ffn2_w_gate, ffn2_w_up, ffn2_w_down, ffn2_post_g):
    stacked = dict(ffn1_pre_g=ffn1_pre_g, ffn1_w_gate=ffn1_w_gate, ffn1_w_up=ffn1_w_up,
                   ffn1_w_down=ffn1_w_down, ffn1_post_g=ffn1_post_g, mix_pre_g=mix_pre_g, w_in=w_in,
                   conv_w=conv_w, attn_out_g=attn_out_g, conv_out_g=conv_out_g, w_out=w_out,
                   mix_post_g=mix_post_g, ffn2_pre_g=ffn2_pre_g, ffn2_w_gate=ffn2_w_gate,
                   ffn2_w_up=ffn2_w_up, ffn2_w_down=ffn2_w_down, ffn2_post_g=ffn2_post_g)
    y_prompt, y_sample = x_prompt, x_sample
    for layer in range(ffn1_pre_g.shape[0]):
        p = {}
        for name, val in stacked.items():
            val = val[layer]
            if name in _MATRICES:
                val = val.astype(BF16)
            elif name != "conv_w":
                val = val.reshape(1, -1)
            p[name] = val
        y_prompt = _encoder_layer(y_prompt, p)
        y_sample = _encoder_layer(y_sample, p)
    return (y_prompt, y_sample)
```

```python
import functools

import jax
import jax.numpy as jnp
from jax import lax
from jax.experimental import pallas as pl
from jax.experimental.pallas import tpu as pltpu

D_MODEL = 1024
D_FF = 2816
HEAD_DIM = 64
N_HEADS = 8
ATTN_WIDTH = N_HEADS * HEAD_DIM
CONV_WIDTH = D_MODEL - ATTN_WIDTH
ROT_DIM = HEAD_DIM // 4
ROPE_THETA = 500000.0
BRANCHES = ((128, 1), (512, 4), (2048, 16))
NORM_EPS = 1e-6

LANES = 128
BF16_SUBLANES = 16
Q_SUB = 128
N_SLABS = ATTN_WIDTH // LANES
ROW_TILE = 512
ATTN_TILE = 2048
FF_CHUNK = 512
VMEM_LIMIT = 48 * 1024 * 1024

F32 = jnp.float32
BF16 = jnp.bfloat16


def _rms(x, g):
    ms = jnp.mean(x * x, axis=-1, keepdims=True)
    return x * lax.rsqrt(ms + NORM_EPS) * g


def _resident(shape):
    zeros = (0,) * len(shape)
    return pl.BlockSpec(shape, lambda *_: zeros, pipeline_mode=pl.Buffered(1))


def _ffn_kernel(x_ref, pre_g_ref, wg_ref, wu_ref, wd_ref, post_g_ref, o_ref, a_scr):
    x = x_ref[0]
    xn = _rms(x, pre_g_ref[...]).astype(BF16)
    for c0 in range(0, D_FF, FF_CHUNK):
        cw = min(FF_CHUNK, D_FF - c0)
        g = jnp.dot(xn, wg_ref[:, c0:c0 + cw], preferred_element_type=F32)
        u = jnp.dot(xn, wu_ref[:, c0:c0 + cw], preferred_element_type=F32)
        a_scr[:, c0:c0 + cw] = (g * jax.nn.sigmoid(g) * u).astype(BF16)
    y = jnp.dot(a_scr[...], wd_ref[...], preferred_element_type=F32)
    o_ref[0] = x + 0.5 * _rms(y, post_g_ref[...])


def _ffn(x, pre_g, wg, wu, wd, post_g):
    B, S, D = x.shape
    tm = ROW_TILE
    tile = pl.BlockSpec((1, tm, D), lambda b, i: (b, i, 0))
    return pl.pallas_call(
        _ffn_kernel,
        grid=(B, S // tm),
        in_specs=[tile, _resident((1, D)), _resident((D, D_FF)), _resident((D, D_FF)),
                  _resident((D_FF, D)), _resident((1, D))],
        out_specs=tile,
        out_shape=jax.ShapeDtypeStruct((B, S, D), F32),
        scratch_shapes=[pltpu.VMEM((tm, D_FF), BF16)],
        compiler_params=pltpu.CompilerParams(
            dimension_semantics=("parallel", "parallel"), vmem_limit_bytes=VMEM_LIMIT),
        name="ffn",
    )(x, pre_g, wg, wu, wd, post_g)


def _mix_in_kernel(h_ref, g_ref, w_ref, cos_ref, sa_ref, sb_ref, *refs):
    n_br = len(BRANCHES)
    q_refs, k_refs, v_refs = refs[0:n_br], refs[n_br:2 * n_br], refs[2 * n_br:3 * n_br]
    cb_ref, cu_ref, slab_scr = refs[3 * n_br:]
    tm = h_ref.shape[1]
    u = _rms(h_ref[0], g_ref[...]).astype(BF16)

    def proj(section):
        c0 = section * ATTN_WIDTH
        return jnp.dot(u, w_ref[:, c0:c0 + ATTN_WIDTH], preferred_element_type=F32)

    def rope(t):
        half = ROT_DIM // 2
        return (t * cos_ref[...] + pltpu.roll(t, LANES - half, 1) * sa_ref[...]
                + pltpu.roll(t, half, 1) * sb_ref[...])

    def emit(t, out_refs, rotary, scale):
        for s in range(N_SLABS):
            ts = t[:, s * LANES:(s + 1) * LANES]
            if rotary:
                ts = rope(ts)
            if scale != 1.0:
                ts = ts * scale
            slab_scr[s] = ts
        for (_, dil), out in zip(BRANCHES, out_refs):
            rows = tm // dil
            for r in range(dil):
                for s in range(N_SLABS):
                    if dil == 1:
                        blk = slab_scr[s]
                    else:
                        blk = slab_scr[s, pl.ds(r, rows, stride=dil), :]
                    out[0, r, :, s * LANES:(s + 1) * LANES] = blk.astype(BF16)

    emit(proj(0), q_refs, True, HEAD_DIM ** -0.5)
    emit(proj(1), k_refs, True, 1.0)
    emit(proj(2), v_refs, False, 1.0)
    cb_ref[0] = proj(3).astype(BF16)
    cu_ref[0] = (proj(4) * proj(5)).astype(BF16)


def _mix_in(h, g, w_in, cos_t, sa_t, sb_t):
    B, S, D = h.shape
    tm = ROW_TILE
    row = lambda w: pl.BlockSpec((1, tm, w), lambda b, i: (b, i, 0))
    tab = pl.BlockSpec((tm, LANES), lambda b, i: (i, 0))
    sub_shapes, sub_specs = [], []
    for _ in range(3):
        for _, dil in BRANCHES:
            sub_shapes.append(jax.ShapeDtypeStruct((B, dil, S // dil, ATTN_WIDTH), BF16))
            sub_specs.append(pl.BlockSpec((1, dil, tm // dil, ATTN_WIDTH), lambda b, i: (b, 0, i, 0)))
    conv_shape = jax.ShapeDtypeStruct((B, S, CONV_WIDTH), BF16)
    return pl.pallas_call(
        _mix_in_kernel,
        grid=(B, S // tm),
        in_specs=[row(D), _resident((1, D)), _resident((D, w_in.shape[1])), tab, tab, tab],
        out_specs=sub_specs + [row(CONV_WIDTH), row(CONV_WIDTH)],
        out_shape=sub_shapes + [conv_shape, conv_shape],
        scratch_shapes=[pltpu.VMEM((N_SLABS, tm, LANES), F32)],
        compiler_params=pltpu.CompilerParams(
            dimension_semantics=("parallel", "parallel"), vmem_limit_bytes=VMEM_LIMIT),
        name="mix_in",
    )(h, g, w_in, cos_t, sa_t, sb_t)


def _attn_kernel(q_ref, k_ref, kl_ref, kr_ref, v_ref, vl_ref, vr_ref, o_ref, lse_ref, kx, vx,
                 *, dil, half, sub_len):
    i = pl.program_id(1)
    tr = q_ref.shape[2]
    nsub = tr // Q_SUB
    span = Q_SUB + 2 * half

    kx[:, 0:half] = kl_ref[0]
    kx[:, half:half + tr] = k_ref[0]
    kx[:, half + tr:] = kr_ref[0]
    vx[:, 0:half] = vl_ref[0]
    vx[:, half:half + tr] = v_ref[0]
    vx[:, half + tr:] = vr_ref[0]

    row = lax.broadcasted_iota(jnp.int32, (Q_SUB, span), 0)
    col = lax.broadcasted_iota(jnp.int32, (Q_SUB, span), 1)
    band_bias = jnp.where((col >= row) & (col <= row + 2 * half), 0.0, -jnp.inf).astype(F32)
    col1 = lax.broadcasted_iota(jnp.int32, (1, span), 1)
    first = lax.broadcasted_iota(jnp.int32, (Q_SUB, LANES), 1) < HEAD_DIM

    def body(j, carry):
        r = j // nsub
        c = j % nsub
        r0 = pl.multiple_of(c * Q_SUB, Q_SUB)
        base = i * tr + c * Q_SUB
        edge_bias = jnp.where((col1 >= half - base) & (col1 < sub_len + half - base), 0.0, -jnp.inf)
        bias = band_bias + edge_bias.astype(F32)
        q2 = q_ref[0, r, pl.ds(r0, Q_SUB), :]
        k2 = kx[r, pl.ds(r0, span), :]
        v2 = vx[r, pl.ds(r0, span), :]
        outs, lses = [], []
        for hh in range(2):
            msk = first if hh == 0 else jnp.logical_not(first)
            qh = jnp.where(msk, q2, jnp.zeros_like(q2))
            s = lax.dot_general(qh, k2, (((1,), (1,)), ((), ())), preferred_element_type=F32) + bias
            m = jnp.max(s, axis=-1, keepdims=True)
            p = jnp.exp(s - m)
            l = jnp.sum(p, axis=-1, keepdims=True)
            o = jnp.dot(p.astype(BF16), v2, preferred_element_type=F32)
            outs.append(o / l)
            lses.append(jnp.broadcast_to(m + jnp.log(l), (Q_SUB, LANES)))
        start = r0 * dil + r
        rows = pl.ds(start, Q_SUB) if dil == 1 else pl.ds(start, Q_SUB, stride=dil)
        o_ref[0, 0, rows, :] = jnp.where(first, outs[0], outs[1])
        lse_ref[0, 0, rows, :] = jnp.where(first, lses[0], lses[1])
        return carry

    lax.fori_loop(0, dil * nsub, body, 0, unroll=8)


def _attn_branch(q, k, v, window, dil):
    B, _, L, _ = q.shape
    S = L * dil
    half = window // (2 * dil)
    T = min(ATTN_TILE, S)
    tr = T // dil
    assert tr % Q_SUB == 0 and L % tr == 0 and tr % half == 0 and half % BF16_SUBLANES == 0
    hb = tr // half
    n_hblk = L // half
    main = pl.BlockSpec((1, dil, tr, LANES), lambda b, i, s: (b, 0, i, s))
    left = pl.BlockSpec((1, dil, half, LANES), lambda b, i, s: (b, 0, jnp.maximum(i * hb - 1, 0), s))
    right = pl.BlockSpec((1, dil, half, LANES),
                         lambda b, i, s: (b, 0, jnp.minimum((i + 1) * hb, n_hblk - 1), s))
    out = pl.BlockSpec((1, 1, T, LANES), lambda b, i, s: (b, s, i, 0))
    out_shape = jax.ShapeDtypeStruct((B, N_SLABS, S, LANES), F32)
    return pl.pallas_call(
        functools.partial(_attn_kernel, dil=dil, half=half, sub_len=L),
        grid=(B, S // T, N_SLABS),
        in_specs=[main, main, left, right, main, left, right],
        out_specs=[out, out],
        out_shape=[out_shape, out_shape],
        scratch_shapes=[pltpu.VMEM((dil, tr + 2 * half, LANES), BF16)] * 2,
        compiler_params=pltpu.CompilerParams(
            dimension_semantics=("parallel", "parallel", "parallel"), vmem_limit_bytes=VMEM_LIMIT),
        name=f"attn_d{dil}",
    )(q, k, k, k, v, v, v)


def _mix_out_kernel(h_ref, *refs):
    n_br = len(BRANCHES)
    o_refs, lse_refs = refs[0:n_br], refs[n_br:2 * n_br]
    (cb_ref, cu_ref, cul_ref, cur_ref, cw_ref, ag_ref, cg_ref, w_ref, pg_ref,
     out_ref, conv_scr) = refs[2 * n_br:]
    i = pl.program_id(1)
    tm = h_ref.shape[1]

    slabs = []
    for s in range(N_SLABS):
        lses = [ref[0, s] for ref in lse_refs]
        top = functools.reduce(jnp.maximum, lses)
        es = [jnp.exp(l - top) for l in lses]
        num = functools.reduce(lambda a, b: a + b, [e * ref[0, s] for e, ref in zip(es, o_refs)])
        slabs.append(num / functools.reduce(lambda a, b: a + b, es))
    attn = jnp.concatenate(slabs, axis=1)

    pad = 8
    lrow = cul_ref[0].astype(F32)[BF16_SUBLANES - 1:BF16_SUBLANES]
    rrow = cur_ref[0].astype(F32)[0:1]
    conv_scr[pad - 1:pad] = jnp.where(i > 0, lrow, 0.0)
    conv_scr[pad:pad + tm] = cu_ref[0].astype(F32)
    conv_scr[pad + tm:pad + tm + 1] = jnp.where(i < pl.num_programs(1) - 1, rrow, 0.0)
    conv = (conv_scr[pad - 1:pad - 1 + tm] * cw_ref[0:1] + conv_scr[pad:pad + tm] * cw_ref[1:2]
            + conv_scr[pad + 1:pad + 1 + tm] * cw_ref[2:3])
    conv = cb_ref[0].astype(F32) * conv

    cat = jnp.concatenate([_rms(attn, ag_ref[...]), _rms(conv, cg_ref[...])], axis=1).astype(BF16)
    mixed = jnp.dot(cat, w_ref[...], preferred_element_type=F32)
    out_ref[0] = h_ref[0] + _rms(mixed, pg_ref[...])


def _mix_out(h, outs, lses, cb, cu, conv_w, attn_g, conv_g, w_out, post_g):
    B, S, D = h.shape
    tm = ROW_TILE
    hb = tm // BF16_SUBLANES
    n_hblk = S // BF16_SUBLANES
    row = lambda w: pl.BlockSpec((1, tm, w), lambda b, i: (b, i, 0))
    slab = pl.BlockSpec((1, N_SLABS, tm, LANES), lambda b, i: (b, 0, i, 0))
    left = pl.BlockSpec((1, BF16_SUBLANES, CONV_WIDTH), lambda b, i: (b, jnp.maximum(i * hb - 1, 0), 0))
    right = pl.BlockSpec((1, BF16_SUBLANES, CONV_WIDTH),
                         lambda b, i: (b, jnp.minimum((i + 1) * hb, n_hblk - 1), 0))
    n_br = len(BRANCHES)
    return pl.pallas_call(
        _mix_out_kernel,
        grid=(B, S // tm),
        in_specs=[row(D)] + [slab] * (2 * n_br) + [
            row(CONV_WIDTH), row(CONV_WIDTH), left, right, _resident(conv_w.shape),
            _resident((1, ATTN_WIDTH)), _resident((1, CONV_WIDTH)), _resident((D, D)), _resident((1, D))],
        out_specs=row(D),
        out_shape=jax.ShapeDtypeStruct((B, S, D), F32),
        scratch_shapes=[pltpu.VMEM((tm + 16, CONV_WIDTH), F32)],
        compiler_params=pltpu.CompilerParams(
            dimension_semantics=("parallel", "parallel"), vmem_limit_bytes=VMEM_LIMIT),
        name="mix_out",
    )(h, *outs, *lses, cb, cu, cu, cu, conv_w, attn_g, conv_g, w_out, post_g)


def _rope_tables(S):
    half = ROT_DIM // 2
    inv_freq = jnp.power(jnp.float32(ROPE_THETA), -jnp.arange(half, dtype=F32) * 2.0 / ROT_DIM)
    ang = jnp.arange(S).astype(F32)[:, None] * inv_freq[None, :]
    cos, sin = jnp.cos(ang), jnp.sin(ang)
    ones = jnp.ones((S, HEAD_DIM - ROT_DIM), F32)
    zeros = jnp.zeros((S, HEAD_DIM - ROT_DIM), F32)
    zh = jnp.zeros((S, half), F32)
    cos_h = jnp.concatenate([cos, cos, ones], axis=1)
    sa_h = jnp.concatenate([-sin, zh, zeros], axis=1)
    sb_h = jnp.concatenate([zh, sin, zeros], axis=1)
    rep = LANES // HEAD_DIM
    return tuple(jnp.tile(t, (1, rep)) for t in (cos_h, sa_h, sb_h))


def _encoder_layer(x, p):
    S = x.shape[1]
    n_br = len(BRANCHES)
    h = _ffn(x, p["ffn1_pre_g"], p["ffn1_w_gate"], p["ffn1_w_up"], p["ffn1_w_down"], p["ffn1_post_g"])
    res = _mix_in(h, p["mix_pre_g"], p["w_in"], *_rope_tables(S))
    qs, ks, vs = res[0:n_br], res[n_br:2 * n_br], res[2 * n_br:3 * n_br]
    cb, cu = res[3 * n_br:]
    outs, lses = [], []
    for (window, dil), q, k, v in zip(BRANCHES, qs, ks, vs):
        o, l = _attn_branch(q, k, v, window, dil)
        outs.append(o)
        lses.append(l)
    h = _mix_out(h, outs, lses, cb, cu, p["conv_w"], p["attn_out_g"], p["conv_out_g"], p["w_out"],
                 p["mix_post_g"])
    return _ffn(h, p["ffn2_pre_g"], p["ffn2_w_gate"], p["ffn2_w_up"], p["ffn2_w_down"], p["ffn2_post_g"])


_MATRICES = ("ffn1_w_gate", "ffn1_w_up", "ffn1_w_down", "w_in", "w_out", "ffn2_w_gate", "ffn2_w_up",
             "ffn2_w_down")


def kernel(x_prompt, x_sample, ffn1_pre_g, ffn1_w_gate, ffn1_w_up, ffn1_w_down, ffn1_post_g, mix_pre_g, w_in, conv_w, attn_out_g, conv_out_g, w_out, mix_post_g, ffn2_pre_g, ffn2_w_gate, ffn2_w_up, ffn2_w_down, ffn2_post_g):
    stacked = dict(ffn1_pre_g=ffn1_pre_g, ffn1_w_gate=ffn1_w_gate, ffn1_w_up=ffn1_w_up,
                   ffn1_w_down=ffn1_w_down, ffn1_post_g=ffn1_post_g, mix_pre_g=mix_pre_g, w_in=w_in,
                   conv_w=conv_w, attn_out_g=attn_out_g, conv_out_g=conv_out_g, w_out=w_out,
                   mix_post_g=mix_post_g, ffn2_pre_g=ffn2_pre_g, ffn2_w_gate=ffn2_w_gate,
                   ffn2_w_up=ffn2_w_up, ffn2_w_down=ffn2_w_down, ffn2_post_g=ffn2_post_g)
    y_prompt, y_sample = x_prompt, x_sample
    for layer in range(ffn1_pre_g.shape[0]):
        p = {}
        for name, val in stacked.items():
            val = val[layer]
            if name in _MATRICES:
                val = val.astype(BF16)
            elif name != "conv_w":
                val = val.reshape(1, -1)
            p[name] = val
        y_prompt = _encoder_layer(y_prompt, p)
        y_sample = _encoder_layer(y_sample, p)
    return (y_prompt, y_sample)
```

```python
import functools

import jax
import jax.numpy as jnp
from jax import lax
from jax.experimental import pallas as pl
from jax.experimental.pallas import tpu as pltpu

D_MODEL = 1024
D_FF = 2816
HEAD_DIM = 64
N_HEADS = 8
ATTN_WIDTH = N_HEADS * HEAD_DIM
CONV_WIDTH = D_MODEL - ATTN_WIDTH
ROT_DIM = HEAD_DIM // 4
ROPE_THETA = 500000.0
BRANCHES = ((128, 1), (512, 4), (2048, 16))
NORM_EPS = 1e-6
LOG2_E = 1.4426950408889634

LANES = 128
BF16_SUBLANES = 16
Q_SUB = 128
N_SLABS = ATTN_WIDTH // LANES
ROW_TILE = 512
ATTN_TILE = 2048
FF_CHUNK = 512
VMEM_LIMIT = 48 * 1024 * 1024

F32 = jnp.float32
BF16 = jnp.bfloat16


def _rms(x, g):
    ms = jnp.mean(x * x, axis=-1, keepdims=True)
    return x * lax.rsqrt(ms + NORM_EPS) * g


def _resident(shape):
    zeros = (0,) * len(shape)
    return pl.BlockSpec(shape, lambda *_: zeros, pipeline_mode=pl.Buffered(1))


def _ffn_kernel(x_ref, pre_g_ref, wg_ref, wu_ref, wd_ref, post_g_ref, o_ref, a_scr):
    x = x_ref[0]
    xn = _rms(x, pre_g_ref[...]).astype(BF16)
    for c0 in range(0, D_FF, FF_CHUNK):
        cw = min(FF_CHUNK, D_FF - c0)
        g = jnp.dot(xn, wg_ref[:, c0:c0 + cw], preferred_element_type=F32)
        u = jnp.dot(xn, wu_ref[:, c0:c0 + cw], preferred_element_type=F32)
        a_scr[:, c0:c0 + cw] = (g * jax.nn.sigmoid(g) * u).astype(BF16)
    y = jnp.dot(a_scr[...], wd_ref[...], preferred_element_type=F32)
    o_ref[0] = x + 0.5 * _rms(y, post_g_ref[...])


def _ffn(x, pre_g, wg, wu, wd, post_g):
    B, S, D = x.shape
    tm = ROW_TILE
    tile = pl.BlockSpec((1, tm, D), lambda b, i: (b, i, 0))
    return pl.pallas_call(
        _ffn_kernel,
        grid=(B, S // tm),
        in_specs=[tile, _resident((1, D)), _resident((D, D_FF)), _resident((D, D_FF)),
                  _resident((D_FF, D)), _resident((1, D))],
        out_specs=tile,
        out_shape=jax.ShapeDtypeStruct((B, S, D), F32),
        scratch_shapes=[pltpu.VMEM((tm, D_FF), BF16)],
        compiler_params=pltpu.CompilerParams(
            dimension_semantics=("parallel", "parallel"), vmem_limit_bytes=VMEM_LIMIT),
        name="ffn",
    )(x, pre_g, wg, wu, wd, post_g)


def _mix_in_kernel(h_ref, g_ref, w_ref, cos_ref, sa_ref, sb_ref, *refs):
    n_br = len(BRANCHES)
    q_refs, k_refs, v_refs = refs[0:n_br], refs[n_br:2 * n_br], refs[2 * n_br:3 * n_br]
    cb_ref, cu_ref = refs[3 * n_br:3 * n_br + 2]
    class_scrs = refs[3 * n_br + 2:]
    tm = h_ref.shape[1]
    u = _rms(h_ref[0], g_ref[...]).astype(BF16)

    def proj(section):
        c0 = section * ATTN_WIDTH
        return jnp.dot(u, w_ref[:, c0:c0 + ATTN_WIDTH], preferred_element_type=F32)

    def rope(t):
        half = ROT_DIM // 2
        return (t * cos_ref[...] + pltpu.roll(t, LANES - half, 1) * sa_ref[...]
                + pltpu.roll(t, half, 1) * sb_ref[...])

    def emit(t, out_refs, rotary, scale):
        for s in range(N_SLABS):
            ts = t[:, s * LANES:(s + 1) * LANES]
            if rotary:
                ts = rope(ts)
            if scale != 1.0:
                ts = ts * scale
            class_scrs[0][s, 0] = ts
        prev_dil = 1
        for level, ((_, dil), out) in enumerate(zip(BRANCHES, out_refs)):
            step = dil // prev_dil
            rows = tm // dil
            for s in range(N_SLABS):
                for a in range(prev_dil):
                    for b in range(step):
                        if step == 1:
                            blk = class_scrs[level][s, a]
                        else:
                            blk = class_scrs[level - 1][s, a, pl.ds(b, rows, stride=step), :]
                            if level < len(class_scrs):
                                class_scrs[level][s, a + prev_dil * b] = blk
                        out[0, a + prev_dil * b, :, s * LANES:(s + 1) * LANES] = blk.astype(BF16)
            prev_dil = dil

    emit(proj(0), q_refs, True, LOG2_E * HEAD_DIM ** -0.5)
    emit(proj(1), k_refs, True, 1.0)
    emit(proj(2), v_refs, False, 1.0)
    cb_ref[0] = proj(3).astype(BF16)
    cu_ref[0] = (proj(4) * proj(5)).astype(BF16)


def _mix_in(h, g, w_in, cos_t, sa_t, sb_t):
    B, S, D = h.shape
    tm = ROW_TILE
    row = lambda w: pl.BlockSpec((1, tm, w), lambda b, i: (b, i, 0))
    tab = pl.BlockSpec((tm, LANES), lambda b, i: (i, 0))
    sub_shapes, sub_specs = [], []
    for _ in range(3):
        for _, dil in BRANCHES:
            sub_shapes.append(jax.ShapeDtypeStruct((B, dil, S // dil, ATTN_WIDTH), BF16))
            sub_specs.append(pl.BlockSpec((1, dil, tm // dil, ATTN_WIDTH), lambda b, i: (b, 0, i, 0)))
    conv_shape = jax.ShapeDtypeStruct((B, S, CONV_WIDTH), BF16)
    return pl.pallas_call(
        _mix_in_kernel,
        grid=(B, S // tm),
        in_specs=[row(D), _resident((1, D)), _resident((D, w_in.shape[1])), tab, tab, tab],
        out_specs=sub_specs + [row(CONV_WIDTH), row(CONV_WIDTH)],
        out_shape=sub_shapes + [conv_shape, conv_shape],
        scratch_shapes=[pltpu.VMEM((N_SLABS, dil, tm // dil, LANES), F32) for _, dil in BRANCHES[:-1]],
        compiler_params=pltpu.CompilerParams(
            dimension_semantics=("parallel", "parallel"), vmem_limit_bytes=VMEM_LIMIT),
        name="mix_in",
    )(h, g, w_in, cos_t, sa_t, sb_t)


def _stat_lanes(hh):
    return (LANES - 1, HEAD_DIM) if hh == 0 else (0, HEAD_DIM - 1)


def _attn_kernel(q_ref, k_ref, kl_ref, kr_ref, v_ref, vl_ref, vr_ref, t_ref, kx, vx,
                 *, dil, half, sub_len):
    i = pl.program_id(1)
    tr = q_ref.shape[2]
    nsub = tr // Q_SUB
    span = Q_SUB + 2 * half

    kx[:, 0:half] = kl_ref[0]
    kx[:, half:half + tr] = k_ref[0]
    kx[:, half + tr:] = kr_ref[0]
    vx[:, 0:half] = vl_ref[0]
    vx[:, half:half + tr] = v_ref[0]
    vx[:, half + tr:] = vr_ref[0]

    row = lax.broadcasted_iota(jnp.int32, (Q_SUB, span), 0)
    col = lax.broadcasted_iota(jnp.int32, (Q_SUB, span), 1)
    band = (col >= row) & (col <= row + 2 * half)
    lane = lax.broadcasted_iota(jnp.int32, (Q_SUB, LANES), 1)
    first = lane < HEAD_DIM
    key_first = lax.broadcasted_iota(jnp.int32, (span, LANES), 1) < HEAD_DIM

    def bias_for(c):
        ok = band
        base = i * tr + c * Q_SUB
        if c == 0:
            ok = ok & (col >= half - base)
        if c == nsub - 1:
            ok = ok & (col < sub_len + half - base)
        return jnp.where(ok, 0.0, -jnp.inf).astype(F32)

    biases = [bias_for(c) if c in (0, nsub - 1) else None for c in range(nsub)]
    if nsub > 2:
        mid = jnp.where(band, 0.0, -jnp.inf).astype(F32)
        biases = [mid if b is None else b for b in biases]

    for r in range(dil):
        for c in range(nsub):
            r0 = c * Q_SUB
            q2 = q_ref[0, r, r0:r0 + Q_SUB, :]
            k2 = kx[r, r0:r0 + span, :]
            v2 = vx[r, r0:r0 + span, :]
            start = r0 * dil + r
            rows = pl.ds(start, Q_SUB) if dil == 1 else pl.ds(start, Q_SUB, stride=dil)
            for hh in range(2):
                own = first if hh == 0 else jnp.logical_not(first)
                qh = jnp.where(own, q2, jnp.zeros_like(q2))
                s = lax.dot_general(qh, k2, (((1,), (1,)), ((), ())), preferred_element_type=F32) + biases[c]
                m = jnp.max(s, axis=-1, keepdims=True)
                p = jnp.exp2(s - m).astype(BF16)
                vh = jnp.where(key_first if hh == 0 else jnp.logical_not(key_first), v2, jnp.ones_like(v2))
                t = jnp.dot(p, vh, preferred_element_type=F32)
                t_ref[0, hh, rows, :] = jnp.where(lane == _stat_lanes(hh)[0], m, t)


def _attn_branch(q, k, v, window, dil):
    B, _, L, _ = q.shape
    S = L * dil
    half = window // (2 * dil)
    T = min(ATTN_TILE, S)
    tr = T // dil
    assert tr % Q_SUB == 0 and L % tr == 0 and tr % half == 0 and half % BF16_SUBLANES == 0
    hb = tr // half
    n_hblk = L // half
    main = pl.BlockSpec((1, dil, tr, LANES), lambda b, i, s: (b, 0, i, s))
    left = pl.BlockSpec((1, dil, half, LANES), lambda b, i, s: (b, 0, jnp.maximum(i * hb - 1, 0), s))
    right = pl.BlockSpec((1, dil, half, LANES),
                         lambda b, i, s: (b, 0, jnp.minimum((i + 1) * hb, n_hblk - 1), s))
    return pl.pallas_call(
        functools.partial(_attn_kernel, dil=dil, half=half, sub_len=L),
        grid=(B, S // T, N_SLABS),
        in_specs=[main, main, left, right, main, left, right],
        out_specs=pl.BlockSpec((1, 2, T, LANES), lambda b, i, s: (b, s, i, 0)),
        out_shape=jax.ShapeDtypeStruct((B, N_HEADS, S, LANES), F32),
        scratch_shapes=[pltpu.VMEM((dil, tr + 2 * half, LANES), BF16)] * 2,
        compiler_params=pltpu.CompilerParams(
            dimension_semantics=("parallel", "parallel", "parallel"), vmem_limit_bytes=VMEM_LIMIT),
        name=f"attn_d{dil}",
    )(q, k, k, k, v, v, v)


def _mix_out_kernel(h_ref, *refs):
    n_br = len(BRANCHES)
    t_refs = refs[0:n_br]
    (cb_ref, cu_ref, cul_ref, cur_ref, cw_ref, ag_ref, cg_ref, w_ref, pg_ref,
     out_ref, conv_scr) = refs[n_br:]
    i = pl.program_id(1)
    tm = h_ref.shape[1]
    first = lax.broadcasted_iota(jnp.int32, (tm, LANES), 1) < HEAD_DIM

    slabs = []
    for s in range(N_SLABS):
        halves = []
        for hh in range(2):
            m_lane, l_lane = _stat_lanes(hh)
            tiles = [ref[0, 2 * s + hh] for ref in t_refs]
            tops = [jnp.broadcast_to(t[:, m_lane:m_lane + 1], (tm, LANES)) for t in tiles]
            top = functools.reduce(jnp.maximum, tops)
            acc = functools.reduce(lambda a, b: a + b, [jnp.exp2(m - top) * t for m, t in zip(tops, tiles)])
            halves.append(acc / jnp.broadcast_to(acc[:, l_lane:l_lane + 1], (tm, LANES)))
        slabs.append(jnp.where(first, halves[0], halves[1]))
    attn = jnp.concatenate(slabs, axis=1)

    pad = 8
    lrow = cul_ref[0].astype(F32)[BF16_SUBLANES - 1:BF16_SUBLANES]
    rrow = cur_ref[0].astype(F32)[0:1]
    conv_scr[pad - 1:pad] = jnp.where(i > 0, lrow, 0.0)
    conv_scr[pad:pad + tm] = cu_ref[0].astype(F32)
    conv_scr[pad + tm:pad + tm + 1] = jnp.where(i < pl.num_programs(1) - 1, rrow, 0.0)
    conv = (conv_scr[pad - 1:pad - 1 + tm] * cw_ref[0:1] + conv_scr[pad:pad + tm] * cw_ref[1:2]
            + conv_scr[pad + 1:pad + 1 + tm] * cw_ref[2:3])
    conv = cb_ref[0].astype(F32) * conv

    cat = jnp.concatenate([_rms(attn, ag_ref[...]), _rms(conv, cg_ref[...])], axis=1).astype(BF16)
    mixed = jnp.dot(cat, w_ref[...], preferred_element_type=F32)
    out_ref[0] = h_ref[0] + _rms(mixed, pg_ref[...])


def _mix_out(h, tiles, cb, cu, conv_w, attn_g, conv_g, w_out, post_g):
    B, S, D = h.shape
    tm = ROW_TILE
    hb = tm // BF16_SUBLANES
    n_hblk = S // BF16_SUBLANES
    row = lambda w: pl.BlockSpec((1, tm, w), lambda b, i: (b, i, 0))
    heads = pl.BlockSpec((1, N_HEADS, tm, LANES), lambda b, i: (b, 0, i, 0))
    left = pl.BlockSpec((1, BF16_SUBLANES, CONV_WIDTH), lambda b, i: (b, jnp.maximum(i * hb - 1, 0), 0))
    right = pl.BlockSpec((1, BF16_SUBLANES, CONV_WIDTH),
                         lambda b, i: (b, jnp.minimum((i + 1) * hb, n_hblk - 1), 0))
    return pl.pallas_call(
        _mix_out_kernel,
        grid=(B, S // tm),
        in_specs=[row(D)] + [heads] * len(BRANCHES) + [
            row(CONV_WIDTH), row(CONV_WIDTH), left, right, _resident(conv_w.shape),
            _resident((1, ATTN_WIDTH)), _resident((1, CONV_WIDTH)), _resident((D, D)), _resident((1, D))],
        out_specs=row(D),
        out_shape=jax.ShapeDtypeStruct((B, S, D), F32),
        scratch_shapes=[pltpu.VMEM((tm + 16, CONV_WIDTH), F32)],
        compiler_params=pltpu.CompilerParams(
            dimension_semantics=("parallel", "parallel"), vmem_limit_bytes=VMEM_LIMIT),
        name="mix_out",
    )(h, *tiles, cb, cu, cu, cu, conv_w, attn_g, conv_g, w_out, post_g)


def _rope_tables(S):
    half = ROT_DIM // 2
    inv_freq = jnp.power(jnp.float32(ROPE_THETA), -jnp.arange(half, dtype=F32) * 2.0 / ROT_DIM)
    ang = jnp.arange(S).astype(F32)[:, None] * inv_freq[None, :]
    cos, sin = jnp.cos(ang), jnp.sin(ang)
    ones = jnp.ones((S, HEAD_DIM - ROT_DIM), F32)
    zeros = jnp.zeros((S, HEAD_DIM - ROT_DIM), F32)
    zh = jnp.zeros((S, half), F32)
    cos_h = jnp.concatenate([cos, cos, ones], axis=1)
    sa_h = jnp.concatenate([-sin, zh, zeros], axis=1)
    sb_h = jnp.concatenate([zh, sin, zeros], axis=1)
    rep = LANES // HEAD_DIM
    return tuple(jnp.tile(t, (1, rep)) for t in (cos_h, sa_h, sb_h))


def _encoder_layer(x, p):
    S = x.shape[1]
    n_br = len(BRANCHES)
    h = _ffn(x, p["ffn1_pre_g"], p["ffn1_w_gate"], p["ffn1_w_up"], p["ffn1_w_down"], p["ffn1_post_g"])
    res = _mix_in(h, p["mix_pre_g"], p["w_in"], *_rope_tables(S))
    qs, ks, vs = res[0:n_br], res[n_br:2 * n_br], res[2 * n_br:3 * n_br]
    cb, cu = res[3 * n_br:]
    tiles = [_attn_branch(q, k, v, window, dil) for (window, dil), q, k, v in zip(BRANCHES, qs, ks, vs)]
    h = _mix_out(h, tiles, cb, cu, p["conv_w"], p["attn_out_g"], p["conv_out_g"], p["w_out"],
                 p["mix_post_g"])
    return _ffn(h, p["ffn2_pre_g"], p["ffn2_w_gate"], p["ffn2_w_up"], p["ffn2_w_down"], p["ffn2_post_g"])


_MATRICES = ("ffn1_w_gate", "ffn1_w_up", "ffn1_w_down", "w_in", "w_out", "ffn2_w_gate", "ffn2_w_up",
             "ffn2_w_down")


def kernel(x_prompt, x_sample, ffn1_pre_g, ffn1_w_gate, ffn1_w_up, ffn1_w_down, ffn1_post_g, mix_pre_g, w_in, conv_w, attn_out_g, conv_out_g, w_out, mix_post_g, ffn2_pre_g, ffn2_w_gate, ffn2_w_up, ffn2_w_down, ffn2_post_g):
    stacked = dict(ffn1_pre_g=ffn1_pre_g, ffn1_w_gate=ffn1_w_gate, ffn1_w_up=ffn1_w_up,
                   ffn1_w_down=ffn1_w_down, ffn1_post_g=ffn1_post_g, mix_pre_g=mix_pre_g, w_in=w_in,
                   conv_w=conv_w, attn_out_g=attn_out_g, conv_out_g=conv_out_g, w_out=w_out,
                   mix_post_g=mix_post_g, ffn2_pre_g=ffn2_pre_g, ffn2_w_gate=ffn2_w_gate,
                   ffn2_w_up=ffn2_w_up, ffn2_w_down=ffn2_w_down, ffn2_post_g=ffn2_post_g)
    y_prompt, y_sample = x_prompt, x_sample
    for layer in range(ffn1_pre_g.shape[0]):
        p = {}
        for name, val in stacked.items():
            val = val[layer]
            if name in _MATRICES:
                val = val.astype(BF16)
            elif name != "conv_w":
                val = val.reshape(1, -1)
            p[name] = val
        y_prompt = _encoder_layer(y_prompt, p)
        y_sample = _encoder_layer(y_sample, p)
    return (y_prompt, y_sample)
```

```python
import functools

import jax
import jax.numpy as jnp
from jax import lax
from jax.experimental import pallas as pl
from jax.experimental.pallas import tpu as pltpu

D_MODEL = 1024
D_FF = 2816
HEAD_DIM = 64
N_HEADS = 8
ATTN_WIDTH = N_HEADS * HEAD_DIM
CONV_WIDTH = D_MODEL - ATTN_WIDTH
ROT_DIM = HEAD_DIM // 4
ROPE_THETA = 500000.0
BRANCHES = ((128, 1), (512, 4), (2048, 16))
NORM_EPS = 1e-6
LOG2_E = 1.4426950408889634

LANES = 128
BF16_SUBLANES = 16
Q_SUB = 128
N_SLABS = ATTN_WIDTH // LANES
ROW_TILE = 512
ATTN_TILE = 2048
FF_CHUNK = 512
VMEM_LIMIT = 48 * 1024 * 1024
MIX_FFN_VMEM_LIMIT = 58 * 1024 * 1024

F32 = jnp.float32
BF16 = jnp.bfloat16


def _rms(x, g):
    ms = jnp.mean(x * x, axis=-1, keepdims=True)
    return x * lax.rsqrt(ms + NORM_EPS) * g


def _resident(shape):
    zeros = (0,) * len(shape)
    return pl.BlockSpec(shape, lambda *_: zeros, pipeline_mode=pl.Buffered(1))


def _ffn_body(x, pre_g_ref, wg_ref, wu_ref, wd_ref, post_g_ref, a_scr):
    xn = _rms(x, pre_g_ref[...]).astype(BF16)
    for c0 in range(0, D_FF, FF_CHUNK):
        cw = min(FF_CHUNK, D_FF - c0)
        g = jnp.dot(xn, wg_ref[:, c0:c0 + cw], preferred_element_type=F32)
        u = jnp.dot(xn, wu_ref[:, c0:c0 + cw], preferred_element_type=F32)
        a_scr[:, c0:c0 + cw] = (g * jax.nn.sigmoid(g) * u).astype(BF16)
    y = jnp.dot(a_scr[...], wd_ref[...], preferred_element_type=F32)
    return x + 0.5 * _rms(y, post_g_ref[...])


def _ffn_kernel(x_ref, pre_g_ref, wg_ref, wu_ref, wd_ref, post_g_ref, o_ref, a_scr):
    o_ref[0] = _ffn_body(x_ref[0], pre_g_ref, wg_ref, wu_ref, wd_ref, post_g_ref, a_scr)


def _ffn(x, pre_g, wg, wu, wd, post_g):
    B, S, D = x.shape
    tm = ROW_TILE
    tile = pl.BlockSpec((1, tm, D), lambda b, i: (b, i, 0))
    return pl.pallas_call(
        _ffn_kernel,
        grid=(B, S // tm),
        in_specs=[tile, _resident((1, D)), _resident((D, D_FF)), _resident((D, D_FF)),
                  _resident((D_FF, D)), _resident((1, D))],
        out_specs=tile,
        out_shape=jax.ShapeDtypeStruct((B, S, D), F32),
        scratch_shapes=[pltpu.VMEM((tm, D_FF), BF16)],
        compiler_params=pltpu.CompilerParams(
            dimension_semantics=("parallel", "parallel"), vmem_limit_bytes=VMEM_LIMIT),
        name="ffn",
    )(x, pre_g, wg, wu, wd, post_g)


def _mix_in_kernel(h_ref, g_ref, w_ref, cos_ref, sa_ref, sb_ref, *refs):
    n_br = len(BRANCHES)
    q_refs, k_refs, v_refs = refs[0:n_br], refs[n_br:2 * n_br], refs[2 * n_br:3 * n_br]
    cb_ref, cu_ref = refs[3 * n_br:3 * n_br + 2]
    class_scrs = refs[3 * n_br + 2:]
    tm = h_ref.shape[1]
    u = _rms(h_ref[0], g_ref[...]).astype(BF16)

    def proj(section):
        c0 = section * ATTN_WIDTH
        return jnp.dot(u, w_ref[:, c0:c0 + ATTN_WIDTH], preferred_element_type=F32)

    def rope(t):
        half = ROT_DIM // 2
        return (t * cos_ref[...] + pltpu.roll(t, LANES - half, 1) * sa_ref[...]
                + pltpu.roll(t, half, 1) * sb_ref[...])

    def emit(t, out_refs, rotary, scale):
        for s in range(N_SLABS):
            ts = t[:, s * LANES:(s + 1) * LANES]
            if rotary:
                ts = rope(ts)
            if scale != 1.0:
                ts = ts * scale
            class_scrs[0][s, 0] = ts
        prev_dil = 1
        for level, ((_, dil), out) in enumerate(zip(BRANCHES, out_refs)):
            step = dil // prev_dil
            rows = tm // dil
            for s in range(N_SLABS):
                for a in range(prev_dil):
                    for b in range(step):
                        if step == 1:
                            blk = class_scrs[level][s, a]
                        else:
                            blk = class_scrs[level - 1][s, a, pl.ds(b, rows, stride=step), :]
                            if level < len(class_scrs):
                                class_scrs[level][s, a + prev_dil * b] = blk
                        out[0, a + prev_dil * b, :, s * LANES:(s + 1) * LANES] = blk.astype(BF16)
            prev_dil = dil

    emit(proj(0), q_refs, True, LOG2_E * HEAD_DIM ** -0.5)
    emit(proj(1), k_refs, True, 1.0)
    emit(proj(2), v_refs, False, 1.0)
    cb_ref[0] = proj(3).astype(BF16)
    cu_ref[0] = (proj(4) * proj(5)).astype(BF16)


def _mix_in(h, g, w_in, cos_t, sa_t, sb_t):
    B, S, D = h.shape
    tm = ROW_TILE
    row = lambda w: pl.BlockSpec((1, tm, w), lambda b, i: (b, i, 0))
    tab = pl.BlockSpec((tm, LANES), lambda b, i: (i, 0))
    sub_shapes, sub_specs = [], []
    for _ in range(3):
        for _, dil in BRANCHES:
            sub_shapes.append(jax.ShapeDtypeStruct((B, dil, S // dil, ATTN_WIDTH), BF16))
            sub_specs.append(pl.BlockSpec((1, dil, tm // dil, ATTN_WIDTH), lambda b, i: (b, 0, i, 0)))
    conv_shape = jax.ShapeDtypeStruct((B, S, CONV_WIDTH), BF16)
    return pl.pallas_call(
        _mix_in_kernel,
        grid=(B, S // tm),
        in_specs=[row(D), _resident((1, D)), _resident((D, w_in.shape[1])), tab, tab, tab],
        out_specs=sub_specs + [row(CONV_WIDTH), row(CONV_WIDTH)],
        out_shape=sub_shapes + [conv_shape, conv_shape],
        scratch_shapes=[pltpu.VMEM((N_SLABS, dil, tm // dil, LANES), F32) for _, dil in BRANCHES[:-1]],
        compiler_params=pltpu.CompilerParams(
            dimension_semantics=("parallel", "parallel"), vmem_limit_bytes=VMEM_LIMIT),
        name="mix_in",
    )(h, g, w_in, cos_t, sa_t, sb_t)


def _stat_lanes(hh):
    return (LANES - 1, HEAD_DIM) if hh == 0 else (0, HEAD_DIM - 1)


def _attn_kernel(q_ref, k_ref, kl_ref, kr_ref, v_ref, vl_ref, vr_ref, t_ref, kx, vx,
                 *, dil, half, sub_len):
    i = pl.program_id(1)
    tr = q_ref.shape[2]
    nsub = tr // Q_SUB
    span = Q_SUB + 2 * half

    kx[:, 0:half] = kl_ref[0]
    kx[:, half:half + tr] = k_ref[0]
    kx[:, half + tr:] = kr_ref[0]
    vx[:, 0:half] = vl_ref[0]
    vx[:, half:half + tr] = v_ref[0]
    vx[:, half + tr:] = vr_ref[0]

    row = lax.broadcasted_iota(jnp.int32, (Q_SUB, span), 0)
    col = lax.broadcasted_iota(jnp.int32, (Q_SUB, span), 1)
    band = (col >= row) & (col <= row + 2 * half)
    lane = lax.broadcasted_iota(jnp.int32, (Q_SUB, LANES), 1)
    first = lane < HEAD_DIM
    key_first = lax.broadcasted_iota(jnp.int32, (span, LANES), 1) < HEAD_DIM

    def bias_for(c):
        ok = band
        base = i * tr + c * Q_SUB
        if c == 0:
            ok = ok & (col >= half - base)
        if c == nsub - 1:
            ok = ok & (col < sub_len + half - base)
        return jnp.where(ok, 0.0, -jnp.inf).astype(F32)

    biases = [bias_for(c) if c in (0, nsub - 1) else None for c in range(nsub)]
    if nsub > 2:
        mid = jnp.where(band, 0.0, -jnp.inf).astype(F32)
        biases = [mid if b is None else b for b in biases]

    for r in range(dil):
        for c in range(nsub):
            r0 = c * Q_SUB
            q2 = q_ref[0, r, r0:r0 + Q_SUB, :]
            k2 = kx[r, r0:r0 + span, :]
            v2 = vx[r, r0:r0 + span, :]
            start = r0 * dil + r
            rows = pl.ds(start, Q_SUB) if dil == 1 else pl.ds(start, Q_SUB, stride=dil)
            for hh in range(2):
                own = first if hh == 0 else jnp.logical_not(first)
                qh = jnp.where(own, q2, jnp.zeros_like(q2))
                s = lax.dot_general(qh, k2, (((1,), (1,)), ((), ())), preferred_element_type=F32) + biases[c]
                m = jnp.max(s, axis=-1, keepdims=True)
                p = jnp.exp2(s - m).astype(BF16)
                vh = jnp.where(key_first if hh == 0 else jnp.logical_not(key_first), v2, jnp.ones_like(v2))
                t = jnp.dot(p, vh, preferred_element_type=F32)
                t_ref[0, hh, rows, :] = jnp.where(lane == _stat_lanes(hh)[0], m, t)


def _attn_branch(q, k, v, window, dil):
    B, _, L, _ = q.shape
    S = L * dil
    half = window // (2 * dil)
    T = min(ATTN_TILE, S)
    tr = T // dil
    assert tr % Q_SUB == 0 and L % tr == 0 and tr % half == 0 and half % BF16_SUBLANES == 0
    hb = tr // half
    n_hblk = L // half
    main = pl.BlockSpec((1, dil, tr, LANES), lambda b, i, s: (b, 0, i, s))
    left = pl.BlockSpec((1, dil, half, LANES), lambda b, i, s: (b, 0, jnp.maximum(i * hb - 1, 0), s))
    right = pl.BlockSpec((1, dil, half, LANES),
                         lambda b, i, s: (b, 0, jnp.minimum((i + 1) * hb, n_hblk - 1), s))
    return pl.pallas_call(
        functools.partial(_attn_kernel, dil=dil, half=half, sub_len=L),
        grid=(B, S // T, N_SLABS),
        in_specs=[main, main, left, right, main, left, right],
        out_specs=pl.BlockSpec((1, 2, T, LANES), lambda b, i, s: (b, s, i, 0)),
        out_shape=jax.ShapeDtypeStruct((B, N_HEADS, S, LANES), F32),
        scratch_shapes=[pltpu.VMEM((dil, tr + 2 * half, LANES), BF16)] * 2,
        compiler_params=pltpu.CompilerParams(
            dimension_semantics=("parallel", "parallel", "parallel"), vmem_limit_bytes=VMEM_LIMIT),
        name=f"attn_d{dil}",
    )(q, k, k, k, v, v, v)


def _mix_out_body(h, t_refs, cb_ref, cu_ref, cul_ref, cur_ref, cw_ref, ag_ref, cg_ref, w_ref, pg_ref,
                  conv_scr, at_start, at_end):
    tm = h.shape[0]
    first = lax.broadcasted_iota(jnp.int32, (tm, LANES), 1) < HEAD_DIM

    slabs = []
    for s in range(N_SLABS):
        halves = []
        for hh in range(2):
            m_lane, l_lane = _stat_lanes(hh)
            tiles = [ref[0, 2 * s + hh] for ref in t_refs]
            tops = [jnp.broadcast_to(t[:, m_lane:m_lane + 1], (tm, LANES)) for t in tiles]
            top = functools.reduce(jnp.maximum, tops)
            acc = functools.reduce(lambda a, b: a + b, [jnp.exp2(m - top) * t for m, t in zip(tops, tiles)])
            halves.append(acc / jnp.broadcast_to(acc[:, l_lane:l_lane + 1], (tm, LANES)))
        slabs.append(jnp.where(first, halves[0], halves[1]))
    attn = jnp.concatenate(slabs, axis=1)

    pad = 8
    lrow = cul_ref[0].astype(F32)[BF16_SUBLANES - 1:BF16_SUBLANES]
    rrow = cur_ref[0].astype(F32)[0:1]
    conv_scr[pad - 1:pad] = jnp.where(at_start, 0.0, lrow)
    conv_scr[pad:pad + tm] = cu_ref[0].astype(F32)
    conv_scr[pad + tm:pad + tm + 1] = jnp.where(at_end, 0.0, rrow)
    conv = (conv_scr[pad - 1:pad - 1 + tm] * cw_ref[0:1] + conv_scr[pad:pad + tm] * cw_ref[1:2]
            + conv_scr[pad + 1:pad + 1 + tm] * cw_ref[2:3])
    conv = cb_ref[0].astype(F32) * conv

    cat = jnp.concatenate([_rms(attn, ag_ref[...]), _rms(conv, cg_ref[...])], axis=1).astype(BF16)
    mixed = jnp.dot(cat, w_ref[...], preferred_element_type=F32)
    return h + _rms(mixed, pg_ref[...])


def _mix_ffn_kernel(h_ref, *refs, tiles_per_seq):
    n_br = len(BRANCHES)
    t_refs = refs[0:n_br]
    (cb_ref, cu_ref, cul_ref, cur_ref, cw_ref, ag_ref, cg_ref, w_ref, pg_ref,
     pre_g_ref, wg_ref, wu_ref, wd_ref, post_g_ref, out_ref, conv_scr, a_scr, slot0, slot1) = refs[n_br:]
    t = pl.program_id(0)
    n_tiles = pl.num_programs(0) - 1
    i = jnp.minimum(t, n_tiles - 1) % tiles_per_seq

    @pl.when(t == 0)
    def _():
        slot1[...] = jnp.zeros_like(slot1)

    def step(mix_slot, ffn_slot):
        out_ref[0] = _ffn_body(ffn_slot[...], pre_g_ref, wg_ref, wu_ref, wd_ref, post_g_ref, a_scr)
        mix_slot[...] = _mix_out_body(h_ref[0], t_refs, cb_ref, cu_ref, cul_ref, cur_ref, cw_ref, ag_ref,
                                      cg_ref, w_ref, pg_ref, conv_scr, i == 0, i == tiles_per_seq - 1)

    pl.when(t % 2 == 0)(lambda: step(slot0, slot1))
    pl.when(t % 2 == 1)(lambda: step(slot1, slot0))


def _mix_ffn(h, tiles, cb, cu, conv_w, attn_g, conv_g, w_out, mix_post_g, pre_g, wg, wu, wd, post_g):
    B, S, D = h.shape
    tm = ROW_TILE
    tps = S // tm
    n_tiles = B * tps
    hb = tm // BF16_SUBLANES
    n_hblk = S // BF16_SUBLANES

    def mix_tile(t):
        t = jnp.minimum(t, n_tiles - 1)
        return t // tps, t % tps

    def ffn_tile(t):
        t = jnp.maximum(t - 1, 0)
        return t // tps, t % tps

    def row(w, which):
        return pl.BlockSpec((1, tm, w), lambda t: (*which(t), 0))

    heads = pl.BlockSpec((1, N_HEADS, tm, LANES), lambda t: (mix_tile(t)[0], 0, mix_tile(t)[1], 0))
    left = pl.BlockSpec((1, BF16_SUBLANES, CONV_WIDTH),
                        lambda t: (mix_tile(t)[0], jnp.maximum(mix_tile(t)[1] * hb - 1, 0), 0))
    right = pl.BlockSpec((1, BF16_SUBLANES, CONV_WIDTH),
                         lambda t: (mix_tile(t)[0], jnp.minimum((mix_tile(t)[1] + 1) * hb, n_hblk - 1), 0))
    return pl.pallas_call(
        functools.partial(_mix_ffn_kernel, tiles_per_seq=tps),
        grid=(n_tiles + 1,),
        in_specs=[row(D, mix_tile)] + [heads] * len(BRANCHES) + [
            row(CONV_WIDTH, mix_tile), row(CONV_WIDTH, mix_tile), left, right, _resident(conv_w.shape),
            _resident((1, ATTN_WIDTH)), _resident((1, CONV_WIDTH)), _resident((D, D)), _resident((1, D)),
            _resident((1, D)), _resident((D, D_FF)), _resident((D, D_FF)), _resident((D_FF, D)),
            _resident((1, D))],
        out_specs=row(D, ffn_tile),
        out_shape=jax.ShapeDtypeStruct((B, S, D), F32),
        scratch_shapes=[pltpu.VMEM((tm + 16, CONV_WIDTH), F32), pltpu.VMEM((tm, D_FF), BF16),
                        pltpu.VMEM((tm, D), F32), pltpu.VMEM((tm, D), F32)],
        compiler_params=pltpu.CompilerParams(
            dimension_semantics=("arbitrary",), vmem_limit_bytes=MIX_FFN_VMEM_LIMIT),
        name="mix_ffn",
    )(h, *tiles, cb, cu, cu, cu, conv_w, attn_g, conv_g, w_out, mix_post_g, pre_g, wg, wu, wd, post_g)


def _rope_tables(S):
    half = ROT_DIM // 2
    inv_freq = jnp.power(jnp.float32(ROPE_THETA), -jnp.arange(half, dtype=F32) * 2.0 / ROT_DIM)
    ang = jnp.arange(S).astype(F32)[:, None] * inv_freq[None, :]
    cos, sin = jnp.cos(ang), jnp.sin(ang)
    ones = jnp.ones((S, HEAD_DIM - ROT_DIM), F32)
    zeros = jnp.zeros((S, HEAD_DIM - ROT_DIM), F32)
    zh = jnp.zeros((S, half), F32)
    cos_h = jnp.concatenate([cos, cos, ones], axis=1)
    sa_h = jnp.concatenate([-sin, zh, zeros], axis=1)
    sb_h = jnp.concatenate([zh, sin, zeros], axis=1)
    rep = LANES // HEAD_DIM
    return tuple(jnp.tile(t, (1, rep)) for t in (cos_h, sa_h, sb_h))


def _encoder_layer(x, p):
    S = x.shape[1]
    n_br = len(BRANCHES)
    h = _ffn(x, p["ffn1_pre_g"], p["ffn1_w_gate"], p["ffn1_w_up"], p["ffn1_w_down"], p["ffn1_post_g"])
    res = _mix_in(h, p["mix_pre_g"], p["w_in"], *_rope_tables(S))
    qs, ks, vs = res[0:n_br], res[n_br:2 * n_br], res[2 * n_br:3 * n_br]
    cb, cu = res[3 * n_br:]
    tiles = [_attn_branch(q, k, v, window, dil) for (window, dil), q, k, v in zip(BRANCHES, qs, ks, vs)]
    return _mix_ffn(h, tiles, cb, cu, p["conv_w"], p["attn_out_g"], p["conv_out_g"], p["w_out"],
                    p["mix_post_g"], p["ffn2_pre_g"], p["ffn2_w_gate"], p["ffn2_w_up"], p["ffn2_w_down"],
                    p["ffn2_post_g"])


_MATRICES = ("ffn1_w_gate", "ffn1_w_up", "ffn1_w_down", "w_in", "w_out", "ffn2_w_gate", "ffn2_w_up",
             "ffn2_w_down")


def kernel(x_prompt, x_sample, ffn1_pre_g, ffn1_w_gate, ffn1_w_up, ffn1_w_down, ffn1_post_g, mix_pre_g, w_in, conv_w, attn_out_g, conv_out_g, w_out, mix_post_g, ffn2_pre_g, ffn2_w_gate, ffn2_w_up, ffn2_w_down, ffn2_post_g):
    stacked = dict(ffn1_pre_g=ffn1_pre_g, ffn1_w_gate=ffn1_w_gate, ffn1_w_up=ffn1_w_up,
                   ffn1_w_down=ffn1_w_down, ffn1_post_g=ffn1_post_g, mix_pre_g=mix_pre_g, w_in=w_in,
                   conv_w=conv_w, attn_out_g=attn_out_g, conv_out_g=conv_out_g, w_out=w_out,
                   mix_post_g=mix_post_g, ffn2_pre_g=ffn2_pre_g, ffn2_w_gate=ffn2_w_gate,
                   ffn2_w_up=ffn2_w_up, ffn2_w_down=ffn2_w_down, ffn2_post_g=ffn2_post_g)
    y_prompt, y_sample = x_prompt, x_sample
    for layer in range(ffn1_pre_g.shape[0]):
        p = {}
        for name, val in stacked.items():
            val = val[layer]
            if name in _MATRICES:
                val = val.astype(BF16)
            elif name != "conv_w":
                val = val.reshape(1, -1)
            p[name] = val
        y_prompt = _encoder_layer(y_prompt, p)
        y_sample = _encoder_layer(y_sample, p)
    return (y_prompt, y_sample)
```

```python
import functools

import jax
import jax.numpy as jnp
from jax import lax
from jax.experimental import pallas as pl
from jax.experimental.pallas import tpu as pltpu

D_MODEL = 1024
D_FF = 2816
HEAD_DIM = 64
N_HEADS = 8
ATTN_WIDTH = N_HEADS * HEAD_DIM
CONV_WIDTH = D_MODEL - ATTN_WIDTH
ROT_DIM = HEAD_DIM // 4
ROPE_THETA = 500000.0
BRANCHES = ((128, 1), (512, 4), (2048, 16))
NORM_EPS = 1e-6
LOG2_E = 1.4426950408889634

LANES = 128
BF16_SUBLANES = 16
Q_SUB = 128
N_SLABS = ATTN_WIDTH // LANES
ROW_TILE = 512
FFN_ROW_TILE = 1024
ATTN_TILE = 2048
FF_CHUNK = 256
VMEM_LIMIT = 48 * 1024 * 1024
MIX_FFN_VMEM_LIMIT = 58 * 1024 * 1024

F32 = jnp.float32
BF16 = jnp.bfloat16


def _rms(x, g):
    ms = jnp.mean(x * x, axis=-1, keepdims=True)
    return x * lax.rsqrt(ms + NORM_EPS) * g


def _resident(shape):
    zeros = (0,) * len(shape)
    return pl.BlockSpec(shape, lambda *_: zeros, pipeline_mode=pl.Buffered(1))


def _ffn_body(x, pre_g_ref, wg_ref, wu_ref, wd_ref, post_g_ref, a_scr):
    xn = _rms(x, pre_g_ref[...]).astype(BF16)
    for c0 in range(0, D_FF, FF_CHUNK):
        cw = min(FF_CHUNK, D_FF - c0)
        g = jnp.dot(xn, wg_ref[:, c0:c0 + cw], preferred_element_type=F32)
        u = jnp.dot(xn, wu_ref[:, c0:c0 + cw], preferred_element_type=F32)
        a_scr[:, c0:c0 + cw] = (g * jax.nn.sigmoid(g) * u).astype(BF16)
    y = jnp.dot(a_scr[...], wd_ref[...], preferred_element_type=F32)
    return x + 0.5 * _rms(y, post_g_ref[...])


def _ffn_kernel(x_ref, pre_g_ref, wg_ref, wu_ref, wd_ref, post_g_ref, o_ref, a_scr):
    o_ref[0] = _ffn_body(x_ref[0], pre_g_ref, wg_ref, wu_ref, wd_ref, post_g_ref, a_scr)


def _ffn(x, pre_g, wg, wu, wd, post_g):
    B, S, D = x.shape
    tm = FFN_ROW_TILE
    tile = pl.BlockSpec((1, tm, D), lambda b, i: (b, i, 0))
    return pl.pallas_call(
        _ffn_kernel,
        grid=(B, S // tm),
        in_specs=[tile, _resident((1, D)), _resident((D, D_FF)), _resident((D, D_FF)),
                  _resident((D_FF, D)), _resident((1, D))],
        out_specs=tile,
        out_shape=jax.ShapeDtypeStruct((B, S, D), F32),
        scratch_shapes=[pltpu.VMEM((tm, D_FF), BF16)],
        compiler_params=pltpu.CompilerParams(
            dimension_semantics=("parallel", "parallel"), vmem_limit_bytes=VMEM_LIMIT),
        name="ffn",
    )(x, pre_g, wg, wu, wd, post_g)


def _mix_in_kernel(h_ref, g_ref, w_ref, cos_ref, sa_ref, sb_ref, *refs):
    n_br = len(BRANCHES)
    q_refs, k_refs, v_refs = refs[0:n_br], refs[n_br:2 * n_br], refs[2 * n_br:3 * n_br]
    cb_ref, cu_ref = refs[3 * n_br:3 * n_br + 2]
    class_scrs = refs[3 * n_br + 2:]
    tm = h_ref.shape[1]
    u = _rms(h_ref[0], g_ref[...]).astype(BF16)

    def proj(section):
        c0 = section * ATTN_WIDTH
        return jnp.dot(u, w_ref[:, c0:c0 + ATTN_WIDTH], preferred_element_type=F32)

    def rope(t):
        half = ROT_DIM // 2
        return (t * cos_ref[...] + pltpu.roll(t, LANES - half, 1) * sa_ref[...]
                + pltpu.roll(t, half, 1) * sb_ref[...])

    def emit(t, out_refs, rotary, scale):
        for s in range(N_SLABS):
            ts = t[:, s * LANES:(s + 1) * LANES]
            if rotary:
                ts = rope(ts)
            if scale != 1.0:
                ts = ts * scale
            class_scrs[0][s, 0] = ts
        prev_dil = 1
        for level, ((_, dil), out) in enumerate(zip(BRANCHES, out_refs)):
            step = dil // prev_dil
            rows = tm // dil
            for s in range(N_SLABS):
                for a in range(prev_dil):
                    for b in range(step):
                        if step == 1:
                            blk = class_scrs[level][s, a]
                        else:
                            blk = class_scrs[level - 1][s, a, pl.ds(b, rows, stride=step), :]
                            if level < len(class_scrs):
                                class_scrs[level][s, a + prev_dil * b] = blk
                        out[0, a + prev_dil * b, :, s * LANES:(s + 1) * LANES] = blk.astype(BF16)
            prev_dil = dil

    emit(proj(0), q_refs, True, LOG2_E * HEAD_DIM ** -0.5)
    emit(proj(1), k_refs, True, 1.0)
    emit(proj(2), v_refs, False, 1.0)
    cb_ref[0] = proj(3).astype(BF16)
    cu_ref[0] = (proj(4) * proj(5)).astype(BF16)


def _mix_in(h, g, w_in, cos_t, sa_t, sb_t):
    B, S, D = h.shape
    tm = ROW_TILE
    row = lambda w: pl.BlockSpec((1, tm, w), lambda b, i: (b, i, 0))
    tab = pl.BlockSpec((tm, LANES), lambda b, i: (i, 0))
    sub_shapes, sub_specs = [], []
    for _ in range(3):
        for _, dil in BRANCHES:
            sub_shapes.append(jax.ShapeDtypeStruct((B, dil, S // dil, ATTN_WIDTH), BF16))
            sub_specs.append(pl.BlockSpec((1, dil, tm // dil, ATTN_WIDTH), lambda b, i: (b, 0, i, 0)))
    conv_shape = jax.ShapeDtypeStruct((B, S, CONV_WIDTH), BF16)
    return pl.pallas_call(
        _mix_in_kernel,
        grid=(B, S // tm),
        in_specs=[row(D), _resident((1, D)), _resident((D, w_in.shape[1])), tab, tab, tab],
        out_specs=sub_specs + [row(CONV_WIDTH), row(CONV_WIDTH)],
        out_shape=sub_shapes + [conv_shape, conv_shape],
        scratch_shapes=[pltpu.VMEM((N_SLABS, dil, tm // dil, LANES), F32) for _, dil in BRANCHES[:-1]],
        compiler_params=pltpu.CompilerParams(
            dimension_semantics=("parallel", "parallel"), vmem_limit_bytes=VMEM_LIMIT),
        name="mix_in",
    )(h, g, w_in, cos_t, sa_t, sb_t)


def _stat_lanes(hh):
    return (LANES - 1, HEAD_DIM) if hh == 0 else (0, HEAD_DIM - 1)


def _attn_kernel(q_ref, k_ref, kl_ref, kr_ref, v_ref, vl_ref, vr_ref, t_ref, *, dil, half, sub_len):
    i = pl.program_id(1)
    tr = q_ref.shape[2]
    nsub = tr // Q_SUB
    span = Q_SUB + 2 * half

    def window(main_ref, left_ref, right_ref, r, c, lanes):
        lo, hi = c * Q_SUB - half, c * Q_SUB + Q_SUB + half
        parts = []
        if lo < 0:
            parts.append(left_ref[0, r, :, lanes])
        parts.append(main_ref[0, r, max(lo, 0):min(hi, tr), lanes])
        if hi > tr:
            parts.append(right_ref[0, r, :, lanes])
        return jnp.concatenate(parts, axis=0) if len(parts) > 1 else parts[0]

    row = lax.broadcasted_iota(jnp.int32, (Q_SUB, span), 0)
    col = lax.broadcasted_iota(jnp.int32, (Q_SUB, span), 1)
    band = (col >= row) & (col <= row + 2 * half)
    lane = lax.broadcasted_iota(jnp.int32, (Q_SUB, LANES), 1)
    first = lane < HEAD_DIM
    key_first = lax.broadcasted_iota(jnp.int32, (span, LANES), 1) < HEAD_DIM

    def bias_for(c):
        ok = band
        base = i * tr + c * Q_SUB
        if c == 0:
            ok = ok & (col >= half - base)
        if c == nsub - 1:
            ok = ok & (col < sub_len + half - base)
        return jnp.where(ok, 0.0, -jnp.inf).astype(F32)

    biases = [bias_for(c) if c in (0, nsub - 1) else None for c in range(nsub)]
    if nsub > 2:
        mid = jnp.where(band, 0.0, -jnp.inf).astype(F32)
        biases = [mid if b is None else b for b in biases]

    for r in range(dil):
        for c in range(nsub):
            r0 = c * Q_SUB
            start = r0 * dil + r
            rows = pl.ds(start, Q_SUB) if dil == 1 else pl.ds(start, Q_SUB, stride=dil)
            for s in range(N_SLABS):
                lanes = slice(s * LANES, (s + 1) * LANES)
                q2 = q_ref[0, r, r0:r0 + Q_SUB, lanes]
                k2 = window(k_ref, kl_ref, kr_ref, r, c, lanes)
                v2 = window(v_ref, vl_ref, vr_ref, r, c, lanes)
                for hh in range(2):
                    own = first if hh == 0 else jnp.logical_not(first)
                    qh = jnp.where(own, q2, jnp.zeros_like(q2))
                    sc = lax.dot_general(qh, k2, (((1,), (1,)), ((), ())), preferred_element_type=F32) + biases[c]
                    m = jnp.max(sc, axis=-1, keepdims=True)
                    p = jnp.exp2(sc - m).astype(BF16)
                    vh = jnp.where(key_first if hh == 0 else jnp.logical_not(key_first), v2, jnp.ones_like(v2))
                    t = jnp.dot(p, vh, preferred_element_type=F32)
                    t_ref[0, 2 * s + hh, rows, :] = jnp.where(lane == _stat_lanes(hh)[0], m, t)


def _attn_branch(q, k, v, window, dil):
    B, _, L, _ = q.shape
    S = L * dil
    half = window // (2 * dil)
    T = min(ATTN_TILE, S)
    tr = T // dil
    assert tr % Q_SUB == 0 and L % tr == 0 and tr % half == 0 and half % BF16_SUBLANES == 0
    hb = tr // half
    n_hblk = L // half
    main = pl.BlockSpec((1, dil, tr, ATTN_WIDTH), lambda b, i: (b, 0, i, 0))
    left = pl.BlockSpec((1, dil, half, ATTN_WIDTH), lambda b, i: (b, 0, jnp.maximum(i * hb - 1, 0), 0))
    right = pl.BlockSpec((1, dil, half, ATTN_WIDTH),
                         lambda b, i: (b, 0, jnp.minimum((i + 1) * hb, n_hblk - 1), 0))
    return pl.pallas_call(
        functools.partial(_attn_kernel, dil=dil, half=half, sub_len=L),
        grid=(B, S // T),
        in_specs=[main, main, left, right, main, left, right],
        out_specs=pl.BlockSpec((1, N_HEADS, T, LANES), lambda b, i: (b, 0, i, 0)),
        out_shape=jax.ShapeDtypeStruct((B, N_HEADS, S, LANES), F32),
        compiler_params=pltpu.CompilerParams(
            dimension_semantics=("parallel", "parallel"), vmem_limit_bytes=VMEM_LIMIT),
        name=f"attn_d{dil}",
    )(q, k, k, k, v, v, v)


def _mix_out_body(h, t_refs, cb_ref, cu_ref, cul_ref, cur_ref, cw_ref, ag_ref, cg_ref, w_ref, pg_ref,
                  conv_scr, at_start, at_end):
    tm = h.shape[0]
    first = lax.broadcasted_iota(jnp.int32, (tm, LANES), 1) < HEAD_DIM

    slabs = []
    for s in range(N_SLABS):
        halves = []
        for hh in range(2):
            m_lane, l_lane = _stat_lanes(hh)
            tiles = [ref[0, 2 * s + hh] for ref in t_refs]
            tops = [jnp.broadcast_to(t[:, m_lane:m_lane + 1], (tm, LANES)) for t in tiles]
            top = functools.reduce(jnp.maximum, tops)
            acc = functools.reduce(lambda a, b: a + b, [jnp.exp2(m - top) * t for m, t in zip(tops, tiles)])
            halves.append(acc / jnp.broadcast_to(acc[:, l_lane:l_lane + 1], (tm, LANES)))
        slabs.append(jnp.where(first, halves[0], halves[1]))
    attn = jnp.concatenate(slabs, axis=1)

    pad = 8
    lrow = cul_ref[0].astype(F32)[BF16_SUBLANES - 1:BF16_SUBLANES]
    rrow = cur_ref[0].astype(F32)[0:1]
    conv_scr[pad - 1:pad] = jnp.where(at_start, 0.0, lrow)
    conv_scr[pad:pad + tm] = cu_ref[0].astype(F32)
    conv_scr[pad + tm:pad + tm + 1] = jnp.where(at_end, 0.0, rrow)
    conv = (conv_scr[pad - 1:pad - 1 + tm] * cw_ref[0:1] + conv_scr[pad:pad + tm] * cw_ref[1:2]
            + conv_scr[pad + 1:pad + 1 + tm] * cw_ref[2:3])
    conv = cb_ref[0].astype(F32) * conv

    cat = jnp.concatenate([_rms(attn, ag_ref[...]), _rms(conv, cg_ref[...])], axis=1).astype(BF16)
    mixed = jnp.dot(cat, w_ref[...], preferred_element_type=F32)
    return h + _rms(mixed, pg_ref[...])


def _mix_ffn_kernel(h_ref, *refs, tiles_per_seq):
    n_br = len(BRANCHES)
    t_refs = refs[0:n_br]
    (cb_ref, cu_ref, cul_ref, cur_ref, cw_ref, ag_ref, cg_ref, w_ref, pg_ref,
     pre_g_ref, wg_ref, wu_ref, wd_ref, post_g_ref, out_ref, conv_scr, a_scr, slot0, slot1) = refs[n_br:]
    t = pl.program_id(0)
    n_tiles = pl.num_programs(0) - 1
    i = jnp.minimum(t, n_tiles - 1) % tiles_per_seq

    @pl.when(t == 0)
    def _():
        slot1[...] = jnp.zeros_like(slot1)

    def step(mix_slot, ffn_slot):
        out_ref[0] = _ffn_body(ffn_slot[...], pre_g_ref, wg_ref, wu_ref, wd_ref, post_g_ref, a_scr)
        mix_slot[...] = _mix_out_body(h_ref[0], t_refs, cb_ref, cu_ref, cul_ref, cur_ref, cw_ref, ag_ref,
                                      cg_ref, w_ref, pg_ref, conv_scr, i == 0, i == tiles_per_seq - 1)

    pl.when(t % 2 == 0)(lambda: step(slot0, slot1))
    pl.when(t % 2 == 1)(lambda: step(slot1, slot0))


def _mix_ffn(h, tiles, cb, cu, conv_w, attn_g, conv_g, w_out, mix_post_g, pre_g, wg, wu, wd, post_g):
    B, S, D = h.shape
    tm = ROW_TILE
    tps = S // tm
    n_tiles = B * tps
    hb = tm // BF16_SUBLANES
    n_hblk = S // BF16_SUBLANES

    def mix_tile(t):
        t = jnp.minimum(t, n_tiles - 1)
        return t // tps, t % tps

    def ffn_tile(t):
        t = jnp.maximum(t - 1, 0)
        return t // tps, t % tps

    def row(w, which):
        return pl.BlockSpec((1, tm, w), lambda t: (*which(t), 0))

    heads = pl.BlockSpec((1, N_HEADS, tm, LANES), lambda t: (mix_tile(t)[0], 0, mix_tile(t)[1], 0))
    left = pl.BlockSpec((1, BF16_SUBLANES, CONV_WIDTH),
                        lambda t: (mix_tile(t)[0], jnp.maximum(mix_tile(t)[1] * hb - 1, 0), 0))
    right = pl.BlockSpec((1, BF16_SUBLANES, CONV_WIDTH),
                         lambda t: (mix_tile(t)[0], jnp.minimum((mix_tile(t)[1] + 1) * hb, n_hblk - 1), 0))
    return pl.pallas_call(
        functools.partial(_mix_ffn_kernel, tiles_per_seq=tps),
        grid=(n_tiles + 1,),
        in_specs=[row(D, mix_tile)] + [heads] * len(BRANCHES) + [
            row(CONV_WIDTH, mix_tile), row(CONV_WIDTH, mix_tile), left, right, _resident(conv_w.shape),
            _resident((1, ATTN_WIDTH)), _resident((1, CONV_WIDTH)), _resident((D, D)), _resident((1, D)),
            _resident((1, D)), _resident((D, D_FF)), _resident((D, D_FF)), _resident((D_FF, D)),
            _resident((1, D))],
        out_specs=row(D, ffn_tile),
        out_shape=jax.ShapeDtypeStruct((B, S, D), F32),
        scratch_shapes=[pltpu.VMEM((tm + 16, CONV_WIDTH), F32), pltpu.VMEM((tm, D_FF), BF16),
                        pltpu.VMEM((tm, D), F32), pltpu.VMEM((tm, D), F32)],
        compiler_params=pltpu.CompilerParams(
            dimension_semantics=("arbitrary",), vmem_limit_bytes=MIX_FFN_VMEM_LIMIT),
        name="mix_ffn",
    )(h, *tiles, cb, cu, cu, cu, conv_w, attn_g, conv_g, w_out, mix_post_g, pre_g, wg, wu, wd, post_g)


def _rope_tables(S):
    half = ROT_DIM // 2
    inv_freq = jnp.power(jnp.float32(ROPE_THETA), -jnp.arange(half, dtype=F32) * 2.0 / ROT_DIM)
    ang = jnp.arange(S).astype(F32)[:, None] * inv_freq[None, :]
    cos, sin = jnp.cos(ang), jnp.sin(ang)
    ones = jnp.ones((S, HEAD_DIM - ROT_DIM), F32)
    zeros = jnp.zeros((S, HEAD_DIM - ROT_DIM), F32)
    zh = jnp.zeros((S, half), F32)
    cos_h = jnp.concatenate([cos, cos, ones], axis=1)
    sa_h = jnp.concatenate([-sin, zh, zeros], axis=1)
    sb_h = jnp.concatenate([zh, sin, zeros], axis=1)
    rep = LANES // HEAD_DIM
    return tuple(jnp.tile(t, (1, rep)) for t in (cos_h, sa_h, sb_h))


def _encoder_layer(x, p):
    S = x.shape[1]
    n_br = len(BRANCHES)
    h = _ffn(x, p["ffn1_pre_g"], p["ffn1_w_gate"], p["ffn1_w_up"], p["ffn1_w_down"], p["ffn1_post_g"])
    res = _mix_in(h, p["mix_pre_g"], p["w_in"], *_rope_tables(S))
    qs, ks, vs = res[0:n_br], res[n_br:2 * n_br], res[2 * n_br:3 * n_br]
    cb, cu = res[3 * n_br:]
    tiles = [_attn_branch(q, k, v, window, dil) for (window, dil), q, k, v in zip(BRANCHES, qs, ks, vs)]
    return _mix_ffn(h, tiles, cb, cu, p["conv_w"], p["attn_out_g"], p["conv_out_g"], p["w_out"],
                    p["mix_post_g"], p["ffn2_pre_g"], p["ffn2_w_gate"], p["ffn2_w_up"], p["ffn2_w_down"],
                    p["ffn2_post_g"])


_MATRICES = ("ffn1_w_gate", "ffn1_w_up", "ffn1_w_down", "w_in", "w_out", "ffn2_w_gate", "ffn2_w_up",
             "ffn2_w_down")


def kernel(x_prompt, x_sample, ffn1_pre_g, ffn1_w_gate, ffn1_w_up, ffn1_w_down, ffn1_post_g, mix_pre_g, w_in, conv_w, attn_out_g, conv_out_g, w_out, mix_post_g, ffn2_pre_g, ffn2_w_gate, ffn2_w_up, ffn2_w_down, ffn2_post_g):
    stacked = dict(ffn1_pre_g=ffn1_pre_g, ffn1_w_gate=ffn1_w_gate, ffn1_w_up=ffn1_w_up,
                   ffn1_w_down=ffn1_w_down, ffn1_post_g=ffn1_post_g, mix_pre_g=mix_pre_g, w_in=w_in,
                   conv_w=conv_w, attn_out_g=attn_out_g, conv_out_g=conv_out_g, w_out=w_out,
                   mix_post_g=mix_post_g, ffn2_pre_g=ffn2_pre_g, ffn2_w_gate=ffn2_w_gate,
                   ffn2_w_up=ffn2_w_up, ffn2_w_down=ffn2_w_down, ffn2_post_g=ffn2_post_g)
    y_prompt, y_sample = x_prompt, x_sample
    for layer in range(ffn1_pre_g.shape[0]):
        p = {}
        for name, val in stacked.items():
            val = val[layer]
            if name in _MATRICES:
                val = val.astype(BF16)
            elif name != "conv_w":
                val = val.reshape(1, -1)
            p[name] = val
        y_prompt = _encoder_layer(y_prompt, p)
        y_sample = _encoder_layer(y_sample, p)
    return (y_prompt, y_sample)
```

```python
import functools

import jax
import jax.numpy as jnp
from jax import lax
from jax.experimental import pallas as pl
from jax.experimental.pallas import tpu as pltpu

D_MODEL = 1024
D_FF = 2816
HEAD_DIM = 64
N_HEADS = 8
ATTN_WIDTH = N_HEADS * HEAD_DIM
CONV_WIDTH = D_MODEL - ATTN_WIDTH
ROT_DIM = HEAD_DIM // 4
ROPE_THETA = 500000.0
BRANCHES = ((128, 1), (512, 4), (2048, 16))
NORM_EPS = 1e-6
LOG2_E = 1.4426950408889634

LANES = 128
BF16_SUBLANES = 16
Q_SUB = 128
N_SLABS = ATTN_WIDTH // LANES
ROW_TILE = 512
FFN_ROW_TILE = 1024
MIX_IN_ROW_TILE = 1024
ATTN_TILE = 2048
FF_CHUNK = 256
VMEM_LIMIT = 48 * 1024 * 1024
MIX_FFN_VMEM_LIMIT = 58 * 1024 * 1024

F32 = jnp.float32
BF16 = jnp.bfloat16


def _rms(x, g):
    ms = jnp.mean(x * x, axis=-1, keepdims=True)
    return x * lax.rsqrt(ms + NORM_EPS) * g


def _resident(shape):
    zeros = (0,) * len(shape)
    return pl.BlockSpec(shape, lambda *_: zeros, pipeline_mode=pl.Buffered(1))


def _ffn_body(x, pre_g_ref, wg_ref, wu_ref, wd_ref, post_g_ref, a_scr):
    xn = _rms(x, pre_g_ref[...]).astype(BF16)
    for c0 in range(0, D_FF, FF_CHUNK):
        cw = min(FF_CHUNK, D_FF - c0)
        g = jnp.dot(xn, wg_ref[:, c0:c0 + cw], preferred_element_type=F32)
        u = jnp.dot(xn, wu_ref[:, c0:c0 + cw], preferred_element_type=F32)
        a_scr[:, c0:c0 + cw] = (g * jax.nn.sigmoid(g) * u).astype(BF16)
    y = jnp.dot(a_scr[...], wd_ref[...], preferred_element_type=F32)
    return x + 0.5 * _rms(y, post_g_ref[...])


def _ffn_kernel(x_ref, pre_g_ref, wg_ref, wu_ref, wd_ref, post_g_ref, o_ref, a_scr):
    o_ref[0] = _ffn_body(x_ref[0], pre_g_ref, wg_ref, wu_ref, wd_ref, post_g_ref, a_scr)


def _ffn(x, pre_g, wg, wu, wd, post_g):
    B, S, D = x.shape
    tm = FFN_ROW_TILE
    tile = pl.BlockSpec((1, tm, D), lambda b, i: (b, i, 0))
    return pl.pallas_call(
        _ffn_kernel,
        grid=(B, S // tm),
        in_specs=[tile, _resident((1, D)), _resident((D, D_FF)), _resident((D, D_FF)),
                  _resident((D_FF, D)), _resident((1, D))],
        out_specs=tile,
        out_shape=jax.ShapeDtypeStruct((B, S, D), F32),
        scratch_shapes=[pltpu.VMEM((tm, D_FF), BF16)],
        compiler_params=pltpu.CompilerParams(
            dimension_semantics=("parallel", "parallel"), vmem_limit_bytes=VMEM_LIMIT),
        name="ffn",
    )(x, pre_g, wg, wu, wd, post_g)


def _mix_in_kernel(h_ref, g_ref, w_ref, cos_ref, sa_ref, sb_ref, *refs):
    n_br = len(BRANCHES)
    q_refs, k_refs, v_refs = refs[0:n_br], refs[n_br:2 * n_br], refs[2 * n_br:3 * n_br]
    cb_ref, cu_ref = refs[3 * n_br:3 * n_br + 2]
    class_scrs = refs[3 * n_br + 2:]
    tm = h_ref.shape[1]
    u = _rms(h_ref[0], g_ref[...]).astype(BF16)

    def proj(section):
        c0 = section * ATTN_WIDTH
        return jnp.dot(u, w_ref[:, c0:c0 + ATTN_WIDTH], preferred_element_type=F32)

    def rope(t):
        half = ROT_DIM // 2
        return (t * cos_ref[...] + pltpu.roll(t, LANES - half, 1) * sa_ref[...]
                + pltpu.roll(t, half, 1) * sb_ref[...])

    def emit(t, out_refs, rotary, scale):
        for s in range(N_SLABS):
            ts = t[:, s * LANES:(s + 1) * LANES]
            if rotary:
                ts = rope(ts)
            if scale != 1.0:
                ts = ts * scale
            class_scrs[0][s, 0] = ts
        prev_dil = 1
        for level, ((_, dil), out) in enumerate(zip(BRANCHES, out_refs)):
            step = dil // prev_dil
            rows = tm // dil
            for s in range(N_SLABS):
                for a in range(prev_dil):
                    for b in range(step):
                        if step == 1:
                            blk = class_scrs[level][s, a]
                        else:
                            blk = class_scrs[level - 1][s, a, pl.ds(b, rows, stride=step), :]
                            if level < len(class_scrs):
                                class_scrs[level][s, a + prev_dil * b] = blk
                        out[0, a + prev_dil * b, :, s * LANES:(s + 1) * LANES] = blk.astype(BF16)
            prev_dil = dil

    emit(proj(0), q_refs, True, LOG2_E * HEAD_DIM ** -0.5)
    emit(proj(1), k_refs, True, 1.0)
    emit(proj(2), v_refs, False, 1.0)
    cb_ref[0] = proj(3).astype(BF16)
    cu_ref[0] = (proj(4) * proj(5)).astype(BF16)


def _mix_in(h, g, w_in, cos_t, sa_t, sb_t):
    B, S, D = h.shape
    tm = MIX_IN_ROW_TILE
    row = lambda w: pl.BlockSpec((1, tm, w), lambda b, i: (b, i, 0))
    tab = pl.BlockSpec((tm, LANES), lambda b, i: (i, 0))
    sub_shapes, sub_specs = [], []
    for _ in range(3):
        for _, dil in BRANCHES:
            sub_shapes.append(jax.ShapeDtypeStruct((B, dil, S // dil, ATTN_WIDTH), BF16))
            sub_specs.append(pl.BlockSpec((1, dil, tm // dil, ATTN_WIDTH), lambda b, i: (b, 0, i, 0)))
    conv_shape = jax.ShapeDtypeStruct((B, S, CONV_WIDTH), BF16)
    return pl.pallas_call(
        _mix_in_kernel,
        grid=(B, S // tm),
        in_specs=[row(D), _resident((1, D)), _resident((D, w_in.shape[1])), tab, tab, tab],
        out_specs=sub_specs + [row(CONV_WIDTH), row(CONV_WIDTH)],
        out_shape=sub_shapes + [conv_shape, conv_shape],
        scratch_shapes=[pltpu.VMEM((N_SLABS, dil, tm // dil, LANES), F32) for _, dil in BRANCHES[:-1]],
        compiler_params=pltpu.CompilerParams(
            dimension_semantics=("parallel", "parallel"), vmem_limit_bytes=VMEM_LIMIT),
        name="mix_in",
    )(h, g, w_in, cos_t, sa_t, sb_t)


def _stat_lanes(hh):
    return (LANES - 1, HEAD_DIM) if hh == 0 else (0, HEAD_DIM - 1)


def _final_stride(dil):
    return 4 if dil % 8 == 0 else 1


def _rows(start, size, stride):
    return pl.ds(start, size) if stride == 1 else pl.ds(start, size, stride=stride)


def _attn_kernel(q_ref, k_ref, kl_ref, kr_ref, v_ref, vl_ref, vr_ref, t_ref, *scatter_scr, dil, half, sub_len):
    i = pl.program_id(1)
    inner = _final_stride(dil)
    outer = dil // inner
    tr = q_ref.shape[2]
    nsub = tr // Q_SUB
    span = Q_SUB + 2 * half

    def window(main_ref, left_ref, right_ref, r, c, lanes):
        lo, hi = c * Q_SUB - half, c * Q_SUB + Q_SUB + half
        parts = []
        if lo < 0:
            parts.append(left_ref[0, r, :, lanes])
        parts.append(main_ref[0, r, max(lo, 0):min(hi, tr), lanes])
        if hi > tr:
            parts.append(right_ref[0, r, :, lanes])
        return jnp.concatenate(parts, axis=0) if len(parts) > 1 else parts[0]

    row = lax.broadcasted_iota(jnp.int32, (Q_SUB, span), 0)
    col = lax.broadcasted_iota(jnp.int32, (Q_SUB, span), 1)
    band = (col >= row) & (col <= row + 2 * half)
    lane = lax.broadcasted_iota(jnp.int32, (Q_SUB, LANES), 1)
    first = lane < HEAD_DIM
    key_first = lax.broadcasted_iota(jnp.int32, (span, LANES), 1) < HEAD_DIM

    def bias_for(c):
        ok = band
        base = i * tr + c * Q_SUB
        if c == 0:
            ok = ok & (col >= half - base)
        if c == nsub - 1:
            ok = ok & (col < sub_len + half - base)
        return jnp.where(ok, 0.0, -jnp.inf).astype(F32)

    biases = [bias_for(c) if c in (0, nsub - 1) else None for c in range(nsub)]
    if nsub > 2:
        mid = jnp.where(band, 0.0, -jnp.inf).astype(F32)
        biases = [mid if b is None else b for b in biases]

    for r in range(dil):
        for c in range(nsub):
            r0 = c * Q_SUB

            def put(h, tile):
                if inner == 1:
                    t_ref[0, h, _rows(r0 * dil + r, Q_SUB, dil), :] = tile
                else:
                    scatter_scr[0][h, r % inner, _rows(r0 * outer + r // inner, Q_SUB, outer), :] = tile

            for s in range(N_SLABS):
                lanes = slice(s * LANES, (s + 1) * LANES)
                q2 = q_ref[0, r, r0:r0 + Q_SUB, lanes]
                k2 = window(k_ref, kl_ref, kr_ref, r, c, lanes)
                v2 = window(v_ref, vl_ref, vr_ref, r, c, lanes)
                for hh in range(2):
                    own = first if hh == 0 else jnp.logical_not(first)
                    qh = jnp.where(own, q2, jnp.zeros_like(q2))
                    sc = lax.dot_general(qh, k2, (((1,), (1,)), ((), ())), preferred_element_type=F32) + biases[c]
                    m = jnp.max(sc, axis=-1, keepdims=True)
                    p = jnp.exp2(sc - m).astype(BF16)
                    vh = jnp.where(key_first if hh == 0 else jnp.logical_not(key_first), v2, jnp.ones_like(v2))
                    t = jnp.dot(p, vh, preferred_element_type=F32)
                    put(2 * s + hh, jnp.where(lane == _stat_lanes(hh)[0], m, t))
    if inner > 1:
        for h in range(N_HEADS):
            for a in range(inner):
                t_ref[0, h, pl.ds(a, tr * outer, stride=inner), :] = scatter_scr[0][h, a]


def _attn_branch(q, k, v, window, dil):
    B, _, L, _ = q.shape
    S = L * dil
    half = window // (2 * dil)
    T = min(ATTN_TILE, S)
    tr = T // dil
    assert tr % Q_SUB == 0 and L % tr == 0 and tr % half == 0 and half % BF16_SUBLANES == 0
    hb = tr // half
    n_hblk = L // half
    inner = _final_stride(dil)
    scatter_scr = [pltpu.VMEM((N_HEADS, inner, T // inner, LANES), F32)] if inner > 1 else []
    main = pl.BlockSpec((1, dil, tr, ATTN_WIDTH), lambda b, i: (b, 0, i, 0))
    left = pl.BlockSpec((1, dil, half, ATTN_WIDTH), lambda b, i: (b, 0, jnp.maximum(i * hb - 1, 0), 0))
    right = pl.BlockSpec((1, dil, half, ATTN_WIDTH),
                         lambda b, i: (b, 0, jnp.minimum((i + 1) * hb, n_hblk - 1), 0))
    return pl.pallas_call(
        functools.partial(_attn_kernel, dil=dil, half=half, sub_len=L),
        grid=(B, S // T),
        in_specs=[main, main, left, right, main, left, right],
        out_specs=pl.BlockSpec((1, N_HEADS, T, LANES), lambda b, i: (b, 0, i, 0)),
        out_shape=jax.ShapeDtypeStruct((B, N_HEADS, S, LANES), F32),
        scratch_shapes=scatter_scr,
        compiler_params=pltpu.CompilerParams(
            dimension_semantics=("parallel", "parallel"), vmem_limit_bytes=VMEM_LIMIT),
        name=f"attn_d{dil}",
    )(q, k, k, k, v, v, v)


def _mix_out_body(h, t_refs, cb_ref, cu_ref, cul_ref, cur_ref, cw_ref, ag_ref, cg_ref, w_ref, pg_ref,
                  conv_scr, at_start, at_end):
    tm = h.shape[0]
    first = lax.broadcasted_iota(jnp.int32, (tm, LANES), 1) < HEAD_DIM

    slabs = []
    for s in range(N_SLABS):
        halves = []
        for hh in range(2):
            m_lane, l_lane = _stat_lanes(hh)
            tiles = [ref[0, 2 * s + hh] for ref in t_refs]
            tops = [jnp.broadcast_to(t[:, m_lane:m_lane + 1], (tm, LANES)) for t in tiles]
            top = functools.reduce(jnp.maximum, tops)
            acc = functools.reduce(lambda a, b: a + b, [jnp.exp2(m - top) * t for m, t in zip(tops, tiles)])
            halves.append(acc / jnp.broadcast_to(acc[:, l_lane:l_lane + 1], (tm, LANES)))
        slabs.append(jnp.where(first, halves[0], halves[1]))
    attn = jnp.concatenate(slabs, axis=1)

    pad = 8
    lrow = cul_ref[0].astype(F32)[BF16_SUBLANES - 1:BF16_SUBLANES]
    rrow = cur_ref[0].astype(F32)[0:1]
    conv_scr[pad - 1:pad] = jnp.where(at_start, 0.0, lrow)
    conv_scr[pad:pad + tm] = cu_ref[0].astype(F32)
    conv_scr[pad + tm:pad + tm + 1] = jnp.where(at_end, 0.0, rrow)
    conv = (conv_scr[pad - 1:pad - 1 + tm] * cw_ref[0:1] + conv_scr[pad:pad + tm] * cw_ref[1:2]
            + conv_scr[pad + 1:pad + 1 + tm] * cw_ref[2:3])
    conv = cb_ref[0].astype(F32) * conv

    cat = jnp.concatenate([_rms(attn, ag_ref[...]), _rms(conv, cg_ref[...])], axis=1).astype(BF16)
    mixed = jnp.dot(cat, w_ref[...], preferred_element_type=F32)
    return h + _rms(mixed, pg_ref[...])


def _mix_ffn_kernel(h_ref, *refs, tiles_per_seq):
    n_br = len(BRANCHES)
    t_refs = refs[0:n_br]
    (cb_ref, cu_ref, cul_ref, cur_ref, cw_ref, ag_ref, cg_ref, w_ref, pg_ref,
     pre_g_ref, wg_ref, wu_ref, wd_ref, post_g_ref, out_ref, conv_scr, a_scr, slot0, slot1) = refs[n_br:]
    t = pl.program_id(0)
    n_tiles = pl.num_programs(0) - 1
    i = jnp.minimum(t, n_tiles - 1) % tiles_per_seq

    @pl.when(t == 0)
    def _():
        slot1[...] = jnp.zeros_like(slot1)

    def step(mix_slot, ffn_slot):
        out_ref[0] = _ffn_body(ffn_slot[...], pre_g_ref, wg_ref, wu_ref, wd_ref, post_g_ref, a_scr)
        mix_slot[...] = _mix_out_body(h_ref[0], t_refs, cb_ref, cu_ref, cul_ref, cur_ref, cw_ref, ag_ref,
                                      cg_ref, w_ref, pg_ref, conv_scr, i == 0, i == tiles_per_seq - 1)

    pl.when(t % 2 == 0)(lambda: step(slot0, slot1))
    pl.when(t % 2 == 1)(lambda: step(slot1, slot0))


def _mix_ffn(h, tiles, cb, cu, conv_w, attn_g, conv_g, w_out, mix_post_g, pre_g, wg, wu, wd, post_g):
    B, S, D = h.shape
    tm = ROW_TILE
    tps = S // tm
    n_tiles = B * tps
    hb = tm // BF16_SUBLANES
    n_hblk = S // BF16_SUBLANES

    def mix_tile(t):
        t = jnp.minimum(t, n_tiles - 1)
        return t // tps, t % tps

    def ffn_tile(t):
        t = jnp.maximum(t - 1, 0)
        return t // tps, t % tps

    def row(w, which):
        return pl.BlockSpec((1, tm, w), lambda t: (*which(t), 0))

    heads = pl.BlockSpec((1, N_HEADS, tm, LANES), lambda t: (mix_tile(t)[0], 0, mix_tile(t)[1], 0))
    left = pl.BlockSpec((1, BF16_SUBLANES, CONV_WIDTH),
                        lambda t: (mix_tile(t)[0], jnp.maximum(mix_tile(t)[1] * hb - 1, 0), 0))
    right = pl.BlockSpec((1, BF16_SUBLANES, CONV_WIDTH),
                         lambda t: (mix_tile(t)[0], jnp.minimum((mix_tile(t)[1] + 1) * hb, n_hblk - 1), 0))
    return pl.pallas_call(
        functools.partial(_mix_ffn_kernel, tiles_per_seq=tps),
        grid=(n_tiles + 1,),
        in_specs=[row(D, mix_tile)] + [heads] * len(BRANCHES) + [
            row(CONV_WIDTH, mix_tile), row(CONV_WIDTH, mix_tile), left, right, _resident(conv_w.shape),
            _resident((1, ATTN_WIDTH)), _resident((1, CONV_WIDTH)), _resident((D, D)), _resident((1, D)),
            _resident((1, D)), _resident((D, D_FF)), _resident((D, D_FF)), _resident((D_FF, D)),
            _resident((1, D))],
        out_specs=row(D, ffn_tile),
        out_shape=jax.ShapeDtypeStruct((B, S, D), F32),
        scratch_shapes=[pltpu.VMEM((tm + 16, CONV_WIDTH), F32), pltpu.VMEM((tm, D_FF), BF16),
                        pltpu.VMEM((tm, D), F32), pltpu.VMEM((tm, D), F32)],
        compiler_params=pltpu.CompilerParams(
            dimension_semantics=("arbitrary",), vmem_limit_bytes=MIX_FFN_VMEM_LIMIT),
        name="mix_ffn",
    )(h, *tiles, cb, cu, cu, cu, conv_w, attn_g, conv_g, w_out, mix_post_g, pre_g, wg, wu, wd, post_g)


def _rope_tables(S):
    half = ROT_DIM // 2
    inv_freq = jnp.power(jnp.float32(ROPE_THETA), -jnp.arange(half, dtype=F32) * 2.0 / ROT_DIM)
    ang = jnp.arange(S).astype(F32)[:, None] * inv_freq[None, :]
    cos, sin = jnp.cos(ang), jnp.sin(ang)
    ones = jnp.ones((S, HEAD_DIM - ROT_DIM), F32)
    zeros = jnp.zeros((S, HEAD_DIM - ROT_DIM), F32)
    zh = jnp.zeros((S, half), F32)
    cos_h = jnp.concatenate([cos, cos, ones], axis=1)
    sa_h = jnp.concatenate([-sin, zh, zeros], axis=1)
    sb_h = jnp.concatenate([zh, sin, zeros], axis=1)
    rep = LANES // HEAD_DIM
    return tuple(jnp.tile(t, (1, rep)) for t in (cos_h, sa_h, sb_h))


def _encoder_layer(x, p):
    S = x.shape[1]
    n_br = len(BRANCHES)
    h = _ffn(x, p["ffn1_pre_g"], p["ffn1_w_gate"], p["ffn1_w_up"], p["ffn1_w_down"], p["ffn1_post_g"])
    res = _mix_in(h, p["mix_pre_g"], p["w_in"], *_rope_tables(S))
    qs, ks, vs = res[0:n_br], res[n_br:2 * n_br], res[2 * n_br:3 * n_br]
    cb, cu = res[3 * n_br:]
    tiles = [_attn_branch(q, k, v, window, dil) for (window, dil), q, k, v in zip(BRANCHES, qs, ks, vs)]
    return _mix_ffn(h, tiles, cb, cu, p["conv_w"], p["attn_out_g"], p["conv_out_g"], p["w_out"],
                    p["mix_post_g"], p["ffn2_pre_g"], p["ffn2_w_gate"], p["ffn2_w_up"], p["ffn2_w_down"],
                    p["ffn2_post_g"])


_MATRICES = ("ffn1_w_gate", "ffn1_w_up", "ffn1_w_down", "w_in", "w_out", "ffn2_w_gate", "ffn2_w_up",
             "ffn2_w_down")


def kernel(x_prompt, x_sample, ffn1_pre_g, ffn1_w_gate, ffn1_w_up, ffn1_w_down, ffn1_post_g, mix_pre_g, w_in, conv_w, attn_out_g, conv_out_g, w_out, mix_post_g, ffn2_pre_g, ffn2_w_gate, ffn2_w_up, ffn2_w_down, ffn2_post_g):
    stacked = dict(ffn1_pre_g=ffn1_pre_g, ffn1_w_gate=ffn1_w_gate, ffn1_w_up=ffn1_w_up,
                   ffn1_w_down=ffn1_w_down, ffn1_post_g=ffn1_post_g, mix_pre_g=mix_pre_g, w_in=w_in,
                   conv_w=conv_w, attn_out_g=attn_out_g, conv_out_g=conv_out_g, w_out=w_out,
                   mix_post_g=mix_post_g, ffn2_pre_g=ffn2_pre_g, ffn2_w_gate=ffn2_w_gate,
                   ffn2_w_up=ffn2_w_up, ffn2_w_down=ffn2_w_down, ffn2_post_g=ffn2_post_g)
    y_prompt, y_sample = x_prompt, x_sample
    for layer in range(ffn1_pre_g.shape[0]):
        p = {}
        for name, val in stacked.items():
            val = val[layer]
            if name in _MATRICES:
                val = val.astype(BF16)
            elif name != "conv_w":
                val = val.reshape(1, -1)
            p[name] = val
        y_prompt = _encoder_layer(y_prompt, p)
        y_sample = _encoder_layer(y_sample, p)
    return (y_prompt, y_sample)
```
